```python
import jax, jax.numpy as jnp
from jax import lax
import numpy as np

D_MODEL = 2048
BATCH = 16
SEQ = 256
DEPTH = 4
DEC_BATCH = 8
DEC_SEQ = 1024
PAST_LEN = 256

GRID_W = 64
W_A = D_MODEL // 4
W_B = D_MODEL // 4
W_C = D_MODEL // 4
W_D = D_MODEL // 4
H_A = 4
DK = W_A // H_A
DV = W_A // H_A
QKV_CONV = 3
DELTA_CHUNK = 64
SGU_CHUNK = 128
G_B = 4
CG_B = W_B // G_B
SCONV = 3
G_D = 4
CG_D = W_D // G_D
POOL_WINDOWS = (2, 4, 8, 16)
D_FF = 11 * D_MODEL // 4
N_EXPERTS = 8
TOP_K = 2
D_FF_EXPERT = 7 * D_MODEL // 2
N_IN = 3 * W_A + W_A + 4 * H_A + 2 * W_B + 3 * W_C + W_D
N_MOD = 6 * D_MODEL
N_DENSE = (DEPTH + 1) // 2
N_MOE = DEPTH // 2

kernel_name = "hybrid_delta_sgu_conv_pool_diffusion_step"


def rmsnorm(x, w, eps=1e-6):
    xf = x.astype(jnp.float32)
    y = xf * lax.rsqrt(jnp.mean(xf * xf, axis=-1, keepdims=True) + eps)
    return (y * w).astype(x.dtype)


def layernorm(x, w, eps=1e-5):
    xf = x.astype(jnp.float32)
    mu = jnp.mean(xf, axis=-1, keepdims=True)
    xc = xf - mu
    y = xc * lax.rsqrt(jnp.mean(xc * xc, axis=-1, keepdims=True) + eps)
    return (y * w).astype(x.dtype)


def l2norm(x, eps=1e-6):
    return x * lax.rsqrt(jnp.sum(x * x, axis=-1, keepdims=True) + eps)


def sincos_2d(rows, cols, dim):
    quarter = dim // 4
    omega = 1.0 / (10000.0 ** (jnp.arange(quarter, dtype=jnp.float32) / quarter))
    r = jnp.arange(rows, dtype=jnp.float32)[:, None] * omega
    cc = jnp.arange(cols, dtype=jnp.float32)[:, None] * omega
    r_emb = jnp.concatenate([jnp.sin(r), jnp.cos(r)], axis=-1)
    c_emb = jnp.concatenate([jnp.sin(cc), jnp.cos(cc)], axis=-1)
    emb = jnp.concatenate([
        jnp.broadcast_to(r_emb[:, None, :], (rows, cols, dim // 2)),
        jnp.broadcast_to(c_emb[None, :, :], (rows, cols, dim // 2))], axis=-1)
    return emb.reshape(rows * cols, dim)


def centred_dwconv(x, w):
    k_w = w.shape[-1]
    t_len = x.shape[1]
    r = k_w // 2
    xp = jnp.pad(x, ((0, 0), (r, r), (0, 0)))
    return sum(xp[:, j:j + t_len] * w[:, j] for j in range(k_w))


def centred_window_mean(x, win):
    bn, t_len, ch = x.shape
    xf = x.astype(jnp.float32)
    cs = jnp.concatenate([jnp.zeros((bn, 1, ch), jnp.float32), jnp.cumsum(xf, axis=1)], axis=1)
    t = jnp.arange(t_len)
    lo = jnp.clip(t - win // 2, 0, t_len)
    hi = jnp.clip(t + win - win // 2, 0, t_len)
    s = cs[:, hi] - cs[:, lo]
    cnt = (hi - lo).astype(jnp.float32)
    return (s / cnt[None, :, None]).astype(x.dtype)


def chunk_gated_delta(q, k, v, g, beta, s0):
    bn, nh, t_len, dk = q.shape
    dv = v.shape[-1]
    c_len = DELTA_CHUNK
    n_ch = t_len // c_len
    q = (q * dk ** -0.5).reshape(bn, nh, n_ch, c_len, dk)
    k = k.reshape(bn, nh, n_ch, c_len, dk)
    v = v.reshape(bn, nh, n_ch, c_len, dv)
    beta = beta.reshape(bn, nh, n_ch, c_len, 1)
    gc = jnp.cumsum(g.reshape(bn, nh, n_ch, c_len), axis=-1)
    incl = jnp.tril(jnp.ones((c_len, c_len), bool))
    strict = jnp.tril(jnp.ones((c_len, c_len), bool), -1)
    decay = jnp.exp(jnp.where(incl, gc[..., :, None] - gc[..., None, :], -jnp.inf))
    kb = k * beta
    lmat = jnp.where(strict, jnp.einsum('bhncd,bhnsd->bhncs', kb, k) * decay, 0.0)
    amat = lmat + jnp.eye(c_len, dtype=q.dtype)
    rhs = jnp.concatenate([v * beta, kb * jnp.exp(gc)[..., None]], axis=-1)
    sol = lax.linalg.triangular_solve(amat, rhs, left_side=True, lower=True)
    u, w = sol[..., :dv], sol[..., dv:]
    attn = jnp.einsum('bhncd,bhnsd->bhncs', q, k) * decay
    qg = q * jnp.exp(gc)[..., None]
    kd = k * jnp.exp(gc[..., -1:] - gc)[..., None]
    glast = jnp.exp(gc[..., -1])[..., None, None]

    def step(s, xs):
        u_n, w_n, qg_n, attn_n, kd_n, gl_n = xs
        v_new = u_n - jnp.einsum('bhcd,bhde->bhce', w_n, s)
        o = jnp.einsum('bhcd,bhde->bhce', qg_n, s) + jnp.einsum('bhcs,bhse->bhce', attn_n, v_new)
        s = s * gl_n + jnp.einsum('bhcd,bhce->bhde', kd_n, v_new)
        return s, o

    xs = tuple(jnp.moveaxis(a, 2, 0) for a in (u, w, qg, attn, kd, glast))
    s_final, o = lax.scan(step, s0, xs)
    o = jnp.moveaxis(o, 0, 2).reshape(bn, nh, t_len, dv)
    return o, s_final


def token_mixers(h, s0f, s0b, lp):
    bn, t_len, _ = h.shape
    proj = h @ lp['w_in']
    o0 = 3 * W_A
    o1 = o0 + W_A
    o2 = o1 + 2 * H_A
    o3 = o2 + 2 * H_A
    o4 = o3 + 2 * W_B
    o5 = o4 + 3 * W_C
    qkv, gate_a, alpha, beta = proj[..., :o0], proj[..., o0:o1], proj[..., o1:o2], proj[..., o2:o3]
    zb, conv_in, p_in = proj[..., o3:o4], proj[..., o4:o5], proj[..., o5:]

    qkv = jax.nn.silu(centred_dwconv(qkv, lp['qkv_conv_w']))
    q, k, v = jnp.split(qkv, 3, axis=-1)

    def heads(a):
        return a.reshape(bn, t_len, H_A, -1).transpose(0, 2, 1, 3).astype(jnp.float32)

    q, k, v = l2norm(heads(q)), l2norm(heads(k)), heads(v)
    alpha = alpha.reshape(bn, t_len, 2, H_A).astype(jnp.float32)
    beta = jax.nn.sigmoid(beta.reshape(bn, t_len, 2, H_A).astype(jnp.float32)).transpose(0, 2, 3, 1)
    g = (-jnp.exp(lp['a_log']) * jax.nn.softplus(alpha + lp['dt_bias'])).transpose(0, 2, 3, 1)
    o_f, s_f = chunk_gated_delta(q, k, v, g[:, 0], beta[:, 0], s0f)
    flip = lambda a: jnp.flip(a, axis=2)
    o_b, s_b = chunk_gated_delta(flip(q), flip(k), flip(v), flip(g[:, 1]), flip(beta[:, 1]), s0b)
    o_a = rmsnorm(o_f + flip(o_b), lp['delta_norm_w'])
    o_a = o_a.transpose(0, 2, 1, 3).reshape(bn, t_len, W_A) * jax.nn.silu(gate_a.astype(jnp.float32))
    out_a = o_a.astype(h.dtype)

    z = jax.nn.gelu(zb)
    u_b, v_b = jnp.split(z, 2, axis=-1)
    v_b = layernorm(v_b, lp['sgu_norm_w']).reshape(bn, t_len // SGU_CHUNK, SGU_CHUNK, G_B, CG_B)
    sp = jnp.einsum('gpq,bnqgc->bnpgc', lp['sgu_w'], v_b) + lp['sgu_b'].T[None, None, :, :, None]
    out_b = u_b * sp.reshape(bn, t_len, W_B)

    b_g, c_g, x_in = jnp.split(conv_in, 3, axis=-1)
    out_c = b_g * centred_dwconv(c_g * x_in, lp['sconv_w'])

    pg = p_in.reshape(bn, t_len, G_D, CG_D)
    pooled = jnp.stack([centred_window_mean(pg[:, :, j], POOL_WINDOWS[j]) for j in range(G_D)], axis=2) - pg
    out_d = jnp.einsum('btgc,gce->btge', pooled, lp['pool_w']).reshape(bn, t_len, W_D) * lp['pool_scale']

    mixed = jnp.concatenate([out_a, out_b, out_c, out_d], axis=-1) @ lp['w_out']
    return mixed, s_f, s_b


def swiglu(h, w_gate, w_up, w_down):
    return (jax.nn.silu(h @ w_gate) * (h @ w_up)) @ w_down


def moe_swiglu(h, router_w, w_gate, w_up, w_down):
    logits = (h @ router_w).astype(jnp.float32)
    top_val, top_idx = lax.top_k(logits, TOP_K)
    top_w = jax.nn.softmax(top_val, axis=-1)
    gates = jnp.sum(jax.nn.one_hot(top_idx, N_EXPERTS, dtype=jnp.float32) * top_w[..., None], axis=-2)
    gates = jnp.moveaxis(gates.astype(h.dtype), -1, 0)

    def expert(acc, xs):
        wg_e, wu_e, wd_e, g_e = xs
        return acc + swiglu(h, wg_e, wu_e, wd_e) * g_e[..., None], None

    acc, _ = lax.scan(expert, jnp.zeros_like(h), (w_gate, w_up, w_down, gates))
    return acc


def trunk_layer(x, cond, s0f, s0b, lp, ffn_fn):
    mod = (jax.nn.silu(cond) @ lp['w_mod'] + lp['b_mod'])[:, None, :]
    sh1, sc1, g1, sh2, sc2, g2 = jnp.split(mod, 6, axis=-1)
    h = rmsnorm(x, lp['norm1']) * (1 + sc1) + sh1
    mixed, s_f, s_b = token_mixers(h, s0f, s0b, lp)
    x = x + g1 * mixed
    h = rmsnorm(x, lp['norm2']) * (1 + sc2) + sh2
    x = x + g2 * ffn_fn(h)
    return x, s_f, s_b


def setup_inputs(seed: int = 0) -> dict:
    key = jax.random.key(seed)
    keys = iter(jax.random.split(key, 64))
    nrm = lambda shape, scale: jax.random.normal(next(keys), shape, jnp.float32) * scale
    dt = jnp.exp(jax.random.uniform(next(keys), (DEPTH, 2, H_A), jnp.float32, np.log(1e-3), np.log(1e-1)))
    return {
        'x_prompt': nrm((BATCH, SEQ, D_MODEL), 1.0),
        'x_sample': nrm((DEC_BATCH, DEC_SEQ, D_MODEL), 1.0),
        'c': nrm((DEC_BATCH, D_MODEL), 1.0),
        'state_delta': nrm((DEC_BATCH, DEPTH, 2, H_A, DK, DV), 0.5),
        'c_ctx': nrm((D_MODEL,), 1.0),
        'norm1_w': 1.0 + nrm((DEPTH, D_MODEL), 0.02),
        'norm2_w': 1.0 + nrm((DEPTH, D_MODEL), 0.02),
        'w_mod': nrm((DEPTH, D_MODEL, N_MOD), 0.5 * D_MODEL ** -0.5),
        'b_mod': nrm((DEPTH, N_MOD), 0.01),
        'w_in': nrm((DEPTH, D_MODEL, N_IN), D_MODEL ** -0.5),
        'w_out': nrm((DEPTH, D_MODEL, D_MODEL), D_MODEL ** -0.5),
        'qkv_conv_w': nrm((DEPTH, 3 * W_A, QKV_CONV), 0.5),
        'delta_a_log': jnp.log(jax.random.uniform(next(keys), (DEPTH, 2, H_A), jnp.float32, 1.0, 16.0)),
        'delta_dt_bias': dt + jnp.log(-jnp.expm1(-dt)),
        'delta_norm_w': 1.0 + nrm((DEPTH, DV), 0.02),
        'sgu_norm_w': 1.0 + nrm((DEPTH, W_B), 0.02),
        'sgu_w': nrm((DEPTH, G_B, SGU_CHUNK, SGU_CHUNK), SGU_CHUNK ** -0.5),
        'sgu_b': 1.0 + nrm((DEPTH, G_B, SGU_CHUNK), 0.01),
        'sconv_w': nrm((DEPTH, W_C, SCONV), 0.5),
        'pool_w': nrm((DEPTH, G_D, CG_D, CG_D), CG_D ** -0.5),
        'pool_scale': 1.0 + nrm((DEPTH, W_D), 0.1),
        'ffn_w_gate': nrm((N_DENSE, D_MODEL, D_FF), D_MODEL ** -0.5),
        'ffn_w_up': nrm((N_DENSE, D_MODEL, D_FF), D_MODEL ** -0.5),
        'ffn_w_down': nrm((N_DENSE, D_FF, D_MODEL), D_FF ** -0.5),
        'router_w': nrm((N_MOE, D_MODEL, N_EXPERTS), D_MODEL ** -0.5),
        'moe_w_gate': nrm((N_MOE, N_EXPERTS, D_MODEL, D_FF_EXPERT), D_MODEL ** -0.5),
        'moe_w_up': nrm((N_MOE, N_EXPERTS, D_MODEL, D_FF_EXPERT), D_MODEL ** -0.5),
        'moe_w_down': nrm((N_MOE, N_EXPERTS, D_FF_EXPERT, D_MODEL), D_FF_EXPERT ** -0.5),
        'final_norm_w': 1.0 + nrm((D_MODEL,), 0.02),
    }


def reference(x_prompt, x_sample, c, state_delta, c_ctx, norm1_w, norm2_w, w_mod, b_mod, w_in, w_out,
              qkv_conv_w, delta_a_log, delta_dt_bias, delta_norm_w, sgu_norm_w, sgu_w, sgu_b, sconv_w,
              pool_w, pool_scale, ffn_w_gate, ffn_w_up, ffn_w_down, router_w, moe_w_gate, moe_w_up,
              moe_w_down, final_norm_w):
    n_lat = x_sample.shape[1]
    rows = n_lat // GRID_W
    xs = x_sample + sincos_2d(rows, GRID_W, D_MODEL).astype(x_sample.dtype)[None]
    xp = x_prompt
    zero_state = jnp.zeros((x_prompt.shape[0], H_A, DK, DV), jnp.float32)
    cache_f32 = state_delta.astype(jnp.float32)
    ctx_cond = c_ctx[None, :]
    ctx_states = []
    for i in range(DEPTH):
        lp = {
            'norm1': norm1_w[i], 'norm2': norm2_w[i], 'w_mod': w_mod[i], 'b_mod': b_mod[i],
            'w_in': w_in[i], 'w_out': w_out[i], 'qkv_conv_w': qkv_conv_w[i],
            'a_log': delta_a_log[i], 'dt_bias': delta_dt_bias[i], 'delta_norm_w': delta_norm_w[i],
            'sgu_norm_w': sgu_norm_w[i], 'sgu_w': sgu_w[i], 'sgu_b': sgu_b[i], 'sconv_w': sconv_w[i],
            'pool_w': pool_w[i], 'pool_scale': pool_scale[i],
        }
        j = i // 2
        if i % 2 == 0:
            ffn_fn = lambda h, j=j: swiglu(h, ffn_w_gate[j], ffn_w_up[j], ffn_w_down[j])
        else:
            ffn_fn = lambda h, j=j: moe_swiglu(h, router_w[j], moe_w_gate[j], moe_w_up[j], moe_w_down[j])
        xp, s_f, s_b = trunk_layer(xp, ctx_cond, zero_state, zero_state, lp, ffn_fn)
        ctx_states.append(jnp.stack([s_f, s_b], axis=1))
        xs, _, _ = trunk_layer(xs, c, cache_f32[:, i, 0], cache_f32[:, i, 1], lp, ffn_fn)
    y_prompt = rmsnorm(xp, final_norm_w)
    y_sample = rmsnorm(xs, final_norm_w)
    new_state_delta = jnp.stack(ctx_states, axis=1).astype(x_prompt.dtype)
    return (y_prompt, y_sample, new_state_delta)
```

```python
import functools

import jax
import jax.numpy as jnp
from jax import lax
from jax.experimental import pallas as pl
from jax.experimental.pallas import tpu as pltpu

F32 = jnp.float32
BF16 = jnp.bfloat16

LANES = 128
ROW_CHUNKS = 16
DELTA_CHUNK = 64
SGU_CHUNK = 128
H_A = 4
POOL_WINDOWS = (2, 4, 8, 16)
TOP_K = 2
VMEM_LIMIT = 56 * 1024 * 1024


def _cparams(*sem):
    return pltpu.CompilerParams(dimension_semantics=sem, vmem_limit_bytes=VMEM_LIMIT)


def _silu(x):
    return x / (1.0 + jnp.exp(-x))


def _sigmoid(x):
    return 1.0 / (1.0 + jnp.exp(-x))


def _softplus(x):
    return jnp.maximum(x, 0.0) + jnp.log1p(jnp.exp(-jnp.abs(x)))


def _gelu_tanh(x):
    return 0.5 * x * (1.0 + jnp.tanh(0.7978845608028654 * (x + 0.044715 * (x * x * x))))


def _rms(x, w, eps=1e-6):
    return x * lax.rsqrt(jnp.mean(x * x, axis=-1, keepdims=True) + eps) * w


def _dot(a, b):
    return jnp.dot(a, b, preferred_element_type=F32)


def _split(a):
    hi = a.astype(BF16)
    lo = (a - hi.astype(F32)).astype(BF16)
    return hi, lo


def _dot3(a, b):
    a_hi, a_lo = _split(a)
    b_hi, b_lo = _split(b)
    return _dot(a_hi, b_hi) + _dot(a_hi, b_lo) + _dot(a_lo, b_hi)


def _bmm(a, b):
    return jnp.einsum('nij,njk->nik', a, b, preferred_element_type=F32)


def _bmm3(a, b):
    a_hi, a_lo = _split(a)
    b_hi, b_lo = _split(b)
    return _bmm(a_hi, b_hi) + _bmm(a_hi, b_lo) + _bmm(a_lo, b_hi)


def _mod_kernel(c_ref, w_ref, b_ref, o_ref):
    a = _silu(c_ref[...]).astype(BF16)
    o_ref[...] = _dot(a, w_ref[...].astype(BF16)) + b_ref[...]


def _mod_call(cond, w_mod, b_mod, tn=1024):
    depth, d, n = w_mod.shape
    rm = cond.shape[0]
    return pl.pallas_call(
        _mod_kernel,
        grid=(depth, n // tn),
        in_specs=[pl.BlockSpec((rm, d), lambda l, j: (0, 0)),
                  pl.BlockSpec((None, d, tn), lambda l, j: (l, 0, j)),
                  pl.BlockSpec((None, 1, tn), lambda l, j: (l, 0, j))],
        out_specs=pl.BlockSpec((None, rm, tn), lambda l, j: (l, 0, j)),
        out_shape=jax.ShapeDtypeStruct((depth, rm, n), F32),
        compiler_params=_cparams("parallel", "parallel"),
        name="adaln_mod",
    )(cond, w_mod, b_mod.reshape(depth, 1, n))


def _mod_row_index(i, tm, mc, sd):
    r0 = i * tm
    return jnp.where(r0 < mc, 0, 1 + (r0 - mc) // sd)


def _inproj_kernel(x_ref, nw_ref, mod_ref, w_ref, wab_ref, proj_ref, ab_ref, h_ref):
    @pl.when(pl.program_id(1) == 0)
    def _():
        h = _rms(x_ref[...], nw_ref[...]) * (1.0 + mod_ref[1:2, :]) + mod_ref[0:1, :]
        hb = h.astype(BF16)
        h_ref[...] = hb
        ab_ref[...] = _dot(hb, wab_ref[...])

    proj_ref[...] = _dot(h_ref[...], w_ref[...])


def _inproj_call(x, nw, mod, w_main, w_ab, mc, sd, tm, tn):
    m, d = x.shape
    n = w_main.shape[1]
    midx = functools.partial(_mod_row_index, tm=tm, mc=mc, sd=sd)
    return pl.pallas_call(
        _inproj_kernel,
        grid=(m // tm, n // tn),
        in_specs=[pl.BlockSpec((tm, d), lambda i, j: (i, 0)),
                  pl.BlockSpec((1, d), lambda i, j: (0, 0)),
                  pl.BlockSpec((None, 6, d), lambda i, j: (midx(i), 0, 0)),
                  pl.BlockSpec((d, tn), lambda i, j: (0, j)),
                  pl.BlockSpec((d, LANES), lambda i, j: (0, 0))],
        out_specs=[pl.BlockSpec((tm, tn), lambda i, j: (i, j)),
                   pl.BlockSpec((tm, LANES), lambda i, j: (i, 0))],
        out_shape=[jax.ShapeDtypeStruct((m, n), F32), jax.ShapeDtypeStruct((m, LANES), F32)],
        scratch_shapes=[pltpu.VMEM((tm, d), BF16)],
        compiler_params=_cparams("parallel", "arbitrary"),
        name="in_proj",
    )(x, nw, mod, w_main, w_ab)


def _shift_rows(x, d, row, t_len):
    if d == 0:
        return x
    y = pltpu.roll(x, (-d) % t_len, axis=0)
    ok = (row + d >= 0) & (row + d < t_len)
    return jnp.where(ok, y, 0.0)


def _conv3(x, w, row, t_len):
    return (_shift_rows(x, -1, row, t_len) * w[0:1, :] + x * w[1:2, :]
            + _shift_rows(x, 1, row, t_len) * w[2:3, :])


def _chunk_cumsum(g, row, t_len, reverse):
    pos = row % DELTA_CHUNK
    s = 1
    while s < DELTA_CHUNK:
        if reverse:
            g = g + jnp.where(pos < DELTA_CHUNK - s, pltpu.roll(g, t_len - s, axis=0), 0.0)
        else:
            g = g + jnp.where(pos >= s, pltpu.roll(g, s, axis=0), 0.0)
        s *= 2
    return g


def _unit_tri_inverse(lmat, r, c):
    eye = jnp.where(r == c, 1.0, 0.0)

    def same_block(b):
        return (r // b) == (c // b)

    x = jnp.where(same_block(8), -lmat, 0.0)
    x2 = _bmm3(x, x)
    x4 = _bmm3(x2, x2)
    p = _bmm3(_bmm3(eye + x, eye + x2), eye + x4)
    b = 8
    while b < DELTA_CHUNK:
        off = jnp.where(same_block(2 * b) & jnp.logical_not(same_block(b)), lmat, 0.0)
        p = p - _bmm3(_bmm3(p, off), p)
        b *= 2
    return p


def _delta_kernel(*refs, t_len, has_s0, write_state):
    (q_ref, k_ref, v_ref, ga_ref, ab_ref, cq_ref, ck_ref, cv_ref, alog_ref, dtb_ref, nw_ref), rest = refs[:11], refs[11:]
    if has_s0:
        s0_ref, rest = rest[0], rest[1:]
    o_ref, rest = rest[0], rest[1:]
    if write_state:
        sout_ref, rest = rest[0], rest[1:]
    (o_scr,) = rest

    head = pl.program_id(1)
    n_ch = t_len // DELTA_CHUNK
    cl = DELTA_CHUNK
    row = lax.broadcasted_iota(jnp.int32, (t_len, LANES), 0)
    lane = lax.broadcasted_iota(jnp.int32, (t_len, LANES), 1)

    def l2n(x):
        return x * lax.rsqrt(jnp.sum(x * x, axis=-1, keepdims=True) + 1e-6)

    q = l2n(_silu(_conv3(q_ref[...], cq_ref[...], row, t_len))) * (LANES ** -0.5)
    k = l2n(_silu(_conv3(k_ref[...], ck_ref[...], row, t_len)))
    v = _silu(_conv3(v_ref[...], cv_ref[...], row, t_len))

    ab = ab_ref[...]
    g_all = -jnp.exp(alog_ref[...]) * _softplus(ab + dtb_ref[...])
    b_all = _sigmoid(ab)

    def column(a, idx):
        col = jnp.sum(jnp.where(lane == idx, a, 0.0), axis=1, keepdims=True)
        return jnp.broadcast_to(col, (t_len, LANES))

    r64 = lax.broadcasted_iota(jnp.int32, (cl, cl), 0)
    c64 = lax.broadcasted_iota(jnp.int32, (cl, cl), 1)

    k3 = k.reshape(n_ch, cl, LANES)
    kb16 = k3.astype(BF16)
    kk = jnp.einsum('ncd,nsd->ncs', kb16, kb16, preferred_element_type=F32)
    qk = jnp.einsum('ncd,nsd->ncs', q.reshape(n_ch, cl, LANES).astype(BF16), kb16, preferred_element_type=F32)

    for direction in range(2):
        rev = direction == 1
        g = column(g_all, direction * H_A + head)
        beta = column(b_all, 2 * H_A + direction * H_A + head)
        gc = _chunk_cumsum(g, row, t_len, rev)
        gc3 = gc.reshape(n_ch, cl, LANES)
        gc_rows = jnp.swapaxes(gc3, 1, 2)[:, :cl, :]
        dmat = gc3[:, :, :cl] - gc_rows
        if rev:
            incl, strict = c64 >= r64, c64 > r64
        else:
            incl, strict = c64 <= r64, c64 < r64
        decay = jnp.exp(jnp.where(incl, dmat, 0.0))
        beta3 = beta.reshape(n_ch, cl, LANES)
        lmat = jnp.where(strict, beta3[:, :, :cl] * kk * decay, 0.0)
        attn = jnp.where(incl, qk * decay, 0.0).astype(BF16)
        tinv = _unit_tri_inverse(lmat, r64, c64)

        egc = jnp.exp(gc)
        kbeta = k * beta
        u3 = _bmm3(tinv, (v * beta).reshape(n_ch, cl, LANES))
        w3 = _bmm3(tinv, (kbeta * egc).reshape(n_ch, cl, LANES)).astype(BF16)
        qg3 = (q * egc).reshape(n_ch, cl, LANES).astype(BF16)
        last = 0 if rev else cl - 1
        glast = gc3[:, last:last + 1, :]
        kd3 = (k3 * jnp.exp(glast - gc3))
        kdt3 = jnp.swapaxes(kd3, 1, 2).astype(BF16)
        gl3 = jnp.exp(glast)

        if has_s0:
            s = s0_ref[direction]
        else:
            s = jnp.zeros((LANES, LANES), F32)
        order = range(n_ch - 1, -1, -1) if rev else range(n_ch)
        for n in order:
            sb = s.astype(BF16)
            v_new = u3[n] - _dot(w3[n], sb)
            o_n = _dot(qg3[n], sb) + _dot(attn[n], v_new.astype(BF16))
            s = s * gl3[n] + _dot(kdt3[n], v_new.astype(BF16))
            sl = pl.ds(n * cl, cl)
            if rev:
                o_scr[sl, :] += o_n
            else:
                o_scr[sl, :] = o_n
        if write_state:
            sout_ref[direction] = s

    o = _rms(o_scr[...], nw_ref[...]) * _silu(ga_ref[...])
    o_ref[...] = o.astype(o_ref.dtype)


def _delta_call(proj, ab, conv_t, alog, dtb, nw, s0, layer, prev, *, t_len, nseq, blk0, write_state):
    m = proj.shape[0]
    has_s0 = s0 is not None
    w_a = H_A * LANES

    def pspec(col0):
        return pl.BlockSpec((t_len, LANES), lambda b, h: (blk0 + b, col0 + h))

    def cspec(col0):
        return pl.BlockSpec((3, LANES), lambda b, h: (0, col0 + h))

    row1 = pl.BlockSpec((1, LANES), lambda b, h: (0, 0))
    in_specs = [pspec(0), pspec(H_A), pspec(2 * H_A), pspec(3 * H_A),
                pl.BlockSpec((t_len, LANES), lambda b, h: (blk0 + b, 0)),
                cspec(0), cspec(H_A), cspec(2 * H_A), row1, row1, row1]
    args = [proj, proj, proj, proj, ab, conv_t, conv_t, conv_t, alog, dtb, nw]
    if has_s0:
        in_specs.append(pl.BlockSpec((None, None, 2, None, LANES, LANES), lambda b, h: (b, layer, 0, h, 0, 0)))
        args.append(s0)
    aliases = {}
    if prev is not None:
        in_specs.append(pl.BlockSpec(memory_space=pl.ANY))
        args.append(prev)
        aliases = {len(args) - 1: 0}
    out_specs = [pl.BlockSpec((t_len, LANES), lambda b, h: (blk0 + b, h))]
    out_shape = [jax.ShapeDtypeStruct((m, w_a), BF16)]
    if write_state:
        out_specs.append(pl.BlockSpec((None, 2, None, LANES, LANES), lambda b, h: (b, 0, h, 0, 0)))
        out_shape.append(jax.ShapeDtypeStruct((nseq, 2, H_A, LANES, LANES), F32))

    def body(*refs):
        if prev is not None:
            n_in = len(args)
            refs = refs[:n_in - 1] + refs[n_in:]
        _delta_kernel(*refs, t_len=t_len, has_s0=has_s0, write_state=write_state)

    return pl.pallas_call(
        body,
        grid=(nseq, H_A),
        in_specs=in_specs,
        out_specs=out_specs,
        out_shape=out_shape,
        scratch_shapes=[pltpu.VMEM((t_len, LANES), F32)],
        input_output_aliases=aliases,
        compiler_params=_cparams("parallel", "parallel"),
        name=f"delta_T{t_len}",
    )(*args)


def _sgu_kernel(z_ref, nw_ref, w_ref, b_ref, o_ref, *, t_len):
    wb = o_ref.shape[1]
    z = _gelu_tanh(z_ref[...])
    u, v = z[:, :wb], z[:, wb:]
    mu = jnp.mean(v, axis=-1, keepdims=True)
    vc = v - mu
    vn = (vc * lax.rsqrt(jnp.mean(vc * vc, axis=-1, keepdims=True) + 1e-5) * nw_ref[...]).astype(BF16)
    for n in range(t_len // SGU_CHUNK):
        rs = slice(n * SGU_CHUNK, (n + 1) * SGU_CHUNK)
        for g in range(wb // LANES):
            cs = slice(g * LANES, (g + 1) * LANES)
            sp = _dot(w_ref[g], vn[rs, cs]) + b_ref[g]
            o_ref[rs, cs] = (u[rs, cs] * sp).astype(o_ref.dtype)


def _sconv_kernel(c_ref, w_ref, o_ref, *, t_len):
    wc = o_ref.shape[1]
    row = lax.broadcasted_iota(jnp.int32, (t_len, wc), 0)
    x = c_ref[...]
    y = _conv3(x[:, wc:2 * wc] * x[:, 2 * wc:], w_ref[...], row, t_len)
    o_ref[...] = (x[:, :wc] * y).astype(o_ref.dtype)


def _pool_kernel(p_ref, w_ref, sc_ref, o_ref, *, t_len):
    row = lax.broadcasted_iota(jnp.int32, (t_len, LANES), 0)
    for j, win in enumerate(POOL_WINDOWS):
        cs = slice(j * LANES, (j + 1) * LANES)
        x = p_ref[:, cs]
        half = win // 2
        acc = x
        for d in range(-half, win - half):
            if d != 0:
                acc = acc + _shift_rows(x, d, row, t_len)
        cnt = (jnp.minimum(row + (win - half), t_len) - jnp.maximum(row - half, 0)).astype(F32)
        pooled = acc / cnt - x
        o_ref[:, cs] = (_dot(pooled.astype(BF16), w_ref[j]) * sc_ref[:, cs]).astype(o_ref.dtype)


def _seq_call(kern, name, proj, col_block, width_in, params, pspecs, prev, *, t_len, nseq, blk0, width_out):
    m = proj.shape[0]
    in_specs = [pl.BlockSpec((t_len, width_in), lambda b: (blk0 + b, col_block))] + pspecs
    args = [proj] + params
    aliases = {}
    if prev is not None:
        in_specs.append(pl.BlockSpec(memory_space=pl.ANY))
        args.append(prev)
        aliases = {len(args) - 1: 0}
    n_in = len(args)

    def body(*refs):
        if prev is not None:
            refs = refs[:n_in - 1] + refs[n_in:]
        kern(*refs, t_len=t_len)

    return pl.pallas_call(
        body,
        grid=(nseq,),
        in_specs=in_specs,
        out_specs=pl.BlockSpec((t_len, width_out), lambda b: (blk0 + b, 0)),
        out_shape=jax.ShapeDtypeStruct((m, width_out), BF16),
        input_output_aliases=aliases,
        compiler_params=_cparams("parallel"),
        name=f"{name}_T{t_len}",
    )(*args)


def _outproj_kernel(x_ref, ma_ref, mb_ref, mc_ref, md_ref, w_ref, mod_ref, o_ref):
    acc = None
    for i, m_ref in enumerate((ma_ref, mb_ref, mc_ref, md_ref)):
        wq = m_ref.shape[1]
        part = _dot(m_ref[...], w_ref[i * wq:(i + 1) * wq, :])
        acc = part if acc is None else acc + part
    o_ref[...] = x_ref[...] + mod_ref[2:3, :] * acc


def _outproj_call(x, mixes, w_out, mod, mc, sd, tm):
    m, d = x.shape
    wq = mixes[0].shape[1]
    midx = functools.partial(_mod_row_index, tm=tm, mc=mc, sd=sd)
    mspec = pl.BlockSpec((tm, wq), lambda i: (i, 0))
    return pl.pallas_call(
        _outproj_kernel,
        grid=(m // tm,),
        in_specs=[pl.BlockSpec((tm, d), lambda i: (i, 0)), mspec, mspec, mspec, mspec,
                  pl.BlockSpec((d, d), lambda i: (0, 0)),
                  pl.BlockSpec((None, 6, d), lambda i: (midx(i), 0, 0))],
        out_specs=pl.BlockSpec((tm, d), lambda i: (i, 0)),
        out_shape=jax.ShapeDtypeStruct((m, d), F32),
        compiler_params=_cparams("parallel"),
        name="out_proj",
    )(x, *mixes, w_out, mod)


def _ffn_kernel(x_ref, nw_ref, mod_ref, wg_ref, wu_ref, wd_ref, o_ref, h_ref, acc_ref):
    j = pl.program_id(1)

    @pl.when(j == 0)
    def _():
        h = _rms(x_ref[...], nw_ref[...]) * (1.0 + mod_ref[4:5, :]) + mod_ref[3:4, :]
        h_ref[...] = h.astype(BF16)
        acc_ref[...] = jnp.zeros_like(acc_ref)

    h = h_ref[...]
    a = (_silu(_dot(h, wg_ref[...])) * _dot(h, wu_ref[...])).astype(BF16)
    acc_ref[...] += _dot(a, wd_ref[...])

    @pl.when(j == pl.num_programs(1) - 1)
    def _():
        o_ref[...] = x_ref[...] + mod_ref[5:6, :] * acc_ref[...]


def _ffn_call(x, nw, mod, wg, wu, wd, mc, sd, tm, tf):
    m, d = x.shape
    f = wg.shape[1]
    midx = functools.partial(_mod_row_index, tm=tm, mc=mc, sd=sd)
    return pl.pallas_call(
        _ffn_kernel,
        grid=(m // tm, f // tf),
        in_specs=[pl.BlockSpec((tm, d), lambda i, j: (i, 0)),
                  pl.BlockSpec((1, d), lambda i, j: (0, 0)),
                  pl.BlockSpec((None, 6, d), lambda i, j: (midx(i), 0, 0)),
                  pl.BlockSpec((d, tf), lambda i, j: (0, j)),
                  pl.BlockSpec((d, tf), lambda i, j: (0, j)),
                  pl.BlockSpec((tf, d), lambda i, j: (j, 0))],
        out_specs=pl.BlockSpec((tm, d), lambda i, j: (i, 0)),
        out_shape=jax.ShapeDtypeStruct((m, d), F32),
        scratch_shapes=[pltpu.VMEM((tm, d), BF16), pltpu.VMEM((tm, d), F32)],
        compiler_params=_cparams("parallel", "arbitrary"),
        name="ffn_dense",
    )(x, nw, mod, wg, wu, wd)


def _route_kernel(x_ref, nw_ref, mod_ref, rw_ref, h_ref, r_ref, *, n_experts):
    tm = x_ref.shape[0]
    h = _rms(x_ref[...], nw_ref[...]) * (1.0 + mod_ref[4:5, :]) + mod_ref[3:4, :]
    for cb in range(ROW_CHUNKS):
        h_ref[pl.ds(cb, tm, stride=ROW_CHUNKS), :] = h[:, cb * LANES:(cb + 1) * LANES]
    lane = lax.broadcasted_iota(jnp.int32, (tm, LANES), 1).astype(F32)
    neg = jnp.float32(-jnp.inf)
    logits = jnp.where(lane < n_experts, _dot3(h, rw_ref[...]), neg)
    m1 = jnp.max(logits, axis=-1, keepdims=True)
    i1 = jnp.min(jnp.where(logits == m1, lane, float(LANES)), axis=-1, keepdims=True)
    rest = jnp.where(lane == i1, neg, logits)
    m2 = jnp.max(rest, axis=-1, keepdims=True)
    i2 = jnp.min(jnp.where(rest == m2, lane, float(LANES)), axis=-1, keepdims=True)
    e2 = jnp.exp(m2 - m1)
    den = 1.0 + e2
    r_ref[...] = jnp.where(lane == 0.0, i1,
                           jnp.where(lane == 1.0, i2,
                                     jnp.where(lane == 2.0, 1.0 / den, jnp.where(lane == 3.0, e2 / den, 0.0))))


def _route_call(x, nw, mod, rw, mc, sd, tm, n_experts):
    m, d = x.shape
    midx = functools.partial(_mod_row_index, tm=tm, mc=mc, sd=sd)
    return pl.pallas_call(
        functools.partial(_route_kernel, n_experts=n_experts),
        grid=(m // tm,),
        in_specs=[pl.BlockSpec((tm, d), lambda i: (i, 0)),
                  pl.BlockSpec((1, d), lambda i: (0, 0)),
                  pl.BlockSpec((None, 6, d), lambda i: (midx(i), 0, 0)),
                  pl.BlockSpec((d, LANES), lambda i: (0, 0))],
        out_specs=[pl.BlockSpec((tm * ROW_CHUNKS, LANES), lambda i: (i, 0)),
                   pl.BlockSpec((tm, LANES), lambda i: (i, 0))],
        out_shape=[jax.ShapeDtypeStruct((m * ROW_CHUNKS, LANES), F32),
                   jax.ShapeDtypeStruct((m, LANES), F32)],
        compiler_params=_cparams("parallel"),
        name="moe_route",
    )(x, nw, mod, rw)


def _row_copy(src_hbm, dst_ref, src_row, dst_row, sem):
    s0 = pl.multiple_of(src_row * ROW_CHUNKS, ROW_CHUNKS)
    d0 = pl.multiple_of(dst_row * ROW_CHUNKS, ROW_CHUNKS)
    return pltpu.make_async_copy(src_hbm.at[pl.ds(s0, ROW_CHUNKS), :], dst_ref.at[pl.ds(d0, ROW_CHUNKS), :], sem)


def _gather_kernel(nrows_ref, idx_ref, src_hbm, dst_hbm, sem, *, rows):
    base = pl.program_id(0) * rows

    @pl.when(base < nrows_ref[0])
    def _():
        def start(r, carry):
            _row_copy(src_hbm, dst_hbm, idx_ref[0, r], base + r, sem).start()
            return carry

        def wait(r, carry):
            _row_copy(src_hbm, dst_hbm, idx_ref[0, r], base + r, sem).wait()
            return carry

        lax.fori_loop(0, rows, start, 0)
        lax.fori_loop(0, rows, wait, 0)


def _gather_call(h_rows, src_idx, nrows, rows):
    p = src_idx.shape[0]
    return pl.pallas_call(
        functools.partial(_gather_kernel, rows=rows),
        grid_spec=pltpu.PrefetchScalarGridSpec(
            num_scalar_prefetch=1,
            grid=(p // rows,),
            in_specs=[pl.BlockSpec((None, 1, rows), lambda i, n: (i, 0, 0), memory_space=pltpu.SMEM),
                      pl.BlockSpec(memory_space=pl.ANY)],
            out_specs=pl.BlockSpec(memory_space=pl.ANY),
            scratch_shapes=[pltpu.SemaphoreType.DMA(())]),
        out_shape=jax.ShapeDtypeStruct((p * ROW_CHUNKS, LANES), F32),
        compiler_params=_cparams("arbitrary"),
        name="moe_gather",
    )(nrows, src_idx.reshape(p // rows, 1, rows), h_rows)


def _experts_kernel(te_ref, nt_ref, xs_ref, wg_ref, wu_ref, wd_ref, o_ref, x_ref, acc_ref):
    i, j = pl.program_id(0), pl.program_id(1)
    tm = x_ref.shape[0]

    @pl.when(i < nt_ref[0])
    def _():
        @pl.when(j == 0)
        def _():
            for cb in range(ROW_CHUNKS):
                x_ref[:, cb * LANES:(cb + 1) * LANES] = xs_ref[pl.ds(cb, tm, stride=ROW_CHUNKS), :].astype(BF16)
            acc_ref[...] = jnp.zeros_like(acc_ref)

        x = x_ref[...]
        a = (_silu(_dot(x, wg_ref[...])) * _dot(x, wu_ref[...])).astype(BF16)
        acc_ref[...] += _dot(a, wd_ref[...])

        @pl.when(j == pl.num_programs(1) - 1)
        def _():
            for cb in range(ROW_CHUNKS):
                o_ref[pl.ds(cb, tm, stride=ROW_CHUNKS), :] = acc_ref[:, cb * LANES:(cb + 1) * LANES]


def _experts_call(tile_e, ntiles, xs, wg, wu, wd, tm, tf):
    n_e, d, f = wg.shape
    p = xs.shape[0] // ROW_CHUNKS
    nf = f // tf

    def row_blk(i, j, te, nt):
        return (jnp.minimum(i, nt[0] - 1), 0)

    def jj(i, j, nt):
        return jnp.where(i < nt[0], j, nf - 1)

    def w_in_blk(i, j, te, nt):
        return (te[i], 0, jj(i, j, nt))

    def w_out_blk(i, j, te, nt):
        return (te[i], jj(i, j, nt), 0)

    return pl.pallas_call(
        _experts_kernel,
        grid_spec=pltpu.PrefetchScalarGridSpec(
            num_scalar_prefetch=2,
            grid=(p // tm, nf),
            in_specs=[pl.BlockSpec((tm * ROW_CHUNKS, LANES), row_blk),
                      pl.BlockSpec((None, d, tf), w_in_blk),
                      pl.BlockSpec((None, d, tf), w_in_blk),
                      pl.BlockSpec((None, tf, d), w_out_blk)],
            out_specs=pl.BlockSpec((tm * ROW_CHUNKS, LANES), row_blk),
            scratch_shapes=[pltpu.VMEM((tm, d), BF16), pltpu.VMEM((tm, d), F32)]),
        out_shape=jax.ShapeDtypeStruct((p * ROW_CHUNKS, LANES), F32),
        compiler_params=_cparams("arbitrary", "arbitrary"),
        name="moe_experts",
    )(tile_e, ntiles, xs, wg, wu, wd)


def _combine_kernel(idx_ref, x_ref, r_ref, mod_ref, ys_hbm, o_ref, buf_ref, sem):
    tc = x_ref.shape[0]
    n_rows = TOP_K * tc

    def start(r, carry):
        _row_copy(ys_hbm, buf_ref, idx_ref[0, r], r, sem).start()
        return carry

    def wait(r, carry):
        _row_copy(ys_hbm, buf_ref, idx_ref[0, r], r, sem).wait()
        return carry

    lax.fori_loop(0, n_rows, start, 0)
    lax.fori_loop(0, n_rows, wait, 0)
    w1 = r_ref[:, 2:3]
    w2 = r_ref[:, 3:4]
    for cb in range(ROW_CHUNKS):
        cs = slice(cb * LANES, (cb + 1) * LANES)
        y1 = buf_ref[pl.ds(cb, tc, stride=ROW_CHUNKS), :]
        y2 = buf_ref[pl.ds(tc * ROW_CHUNKS + cb, tc, stride=ROW_CHUNKS), :]
        o_ref[:, cs] = x_ref[:, cs] + mod_ref[5:6, cs] * (y1 * w1 + y2 * w2)


def _combine_call(dest, x, route, mod, ys, mc, sd, tc):
    m, d = x.shape
    midx = functools.partial(_mod_row_index, tm=tc, mc=mc, sd=sd)
    return pl.pallas_call(
        _combine_kernel,
        grid=(m // tc,),
        in_specs=[pl.BlockSpec((None, 1, TOP_K * tc), lambda i: (i, 0, 0), memory_space=pltpu.SMEM),
                  pl.BlockSpec((tc, d), lambda i: (i, 0)),
                  pl.BlockSpec((tc, LANES), lambda i: (i, 0)),
                  pl.BlockSpec((None, 6, d), lambda i: (midx(i), 0, 0)),
                  pl.BlockSpec(memory_space=pl.ANY)],
        out_specs=pl.BlockSpec((tc, d), lambda i: (i, 0)),
        out_shape=jax.ShapeDtypeStruct((m, d), F32),
        scratch_shapes=[pltpu.VMEM((TOP_K * tc * ROW_CHUNKS, LANES), F32), pltpu.SemaphoreType.DMA(())],
        compiler_params=_cparams("arbitrary"),
        name="moe_combine",
    )(dest, x, route, mod, ys)


def _moe_plan(route, n_experts, tm, n_tiles, tc):
    m = route.shape[0]
    e_flat = jnp.concatenate([route[:, 0], route[:, 1]]).astype(jnp.int32)
    onehot = (e_flat[:, None] == jnp.arange(n_experts, dtype=jnp.int32)[None, :]).astype(jnp.int32)
    csum = jnp.cumsum(onehot, axis=0)
    cnt = csum[-1]
    rank = jnp.take_along_axis(csum, e_flat[:, None], axis=1)[:, 0] - 1
    gsz = ((cnt + tm - 1) // tm) * tm
    off_end = jnp.cumsum(gsz)
    dest = (off_end - gsz)[e_flat] + rank
    n_rows = off_end[-1]
    n_used = n_rows // tm
    tiles = jnp.arange(n_tiles, dtype=jnp.int32)
    tile_e = jnp.minimum(jnp.searchsorted(off_end, tiles * tm, side='right'), n_experts - 1).astype(jnp.int32)
    tile_e = jnp.where(tiles < n_used, tile_e, tile_e[jnp.maximum(n_used - 1, 0)])
    tok = jnp.tile(jnp.arange(m, dtype=jnp.int32), TOP_K)
    src = jnp.zeros((n_tiles * tm,), jnp.int32).at[dest].set(tok)
    dest_tiles = jnp.concatenate([dest[:m].reshape(m // tc, 1, tc), dest[m:].reshape(m // tc, 1, tc)], axis=2)
    return src, dest_tiles, tile_e, n_used.reshape(1).astype(jnp.int32), n_rows.reshape(1).astype(jnp.int32)


def _moe_layer(x, nw, mod, rw, wg, wu, wd, mc, sd, tm_route, tm, tf, tc, gather_rows):
    m, d = x.shape
    n_experts = wg.shape[0]
    p_rows = TOP_K * m + n_experts * tm
    p_rows = -(-p_rows // gather_rows) * gather_rows
    n_tiles = p_rows // tm
    h_rows, route = _route_call(x, nw, mod, rw, mc, sd, tm_route, n_experts)
    src, dest_tiles, tile_e, n_used, n_rows = _moe_plan(route, n_experts, tm, n_tiles, tc)
    xs = _gather_call(h_rows, src, n_rows, gather_rows)
    ys = _experts_call(tile_e, n_used, xs, wg, wu, wd, tm, tf)
    return _combine_call(dest_tiles, x, route, mod, ys, mc, sd, tc)


def _final_kernel(x_ref, w_ref, o_ref):
    o_ref[...] = _rms(x_ref[...], w_ref[...])


def _final_call(x, w, tm):
    m, d = x.shape
    return pl.pallas_call(
        _final_kernel,
        grid=(m // tm,),
        in_specs=[pl.BlockSpec((tm, d), lambda i: (i, 0)), pl.BlockSpec((1, d), lambda i: (0, 0))],
        out_specs=pl.BlockSpec((tm, d), lambda i: (i, 0)),
        out_shape=jax.ShapeDtypeStruct((m, d), F32),
        compiler_params=_cparams("parallel"),
        name="final_norm",
    )(x, w)


def _sincos_2d(rows, cols, dim):
    quarter = dim // 4
    omega = 1.0 / (10000.0 ** (jnp.arange(quarter, dtype=F32) / quarter))
    r = jnp.arange(rows, dtype=F32)[:, None] * omega
    cc = jnp.arange(cols, dtype=F32)[:, None] * omega
    r_emb = jnp.concatenate([jnp.sin(r), jnp.cos(r)], axis=-1)
    c_emb = jnp.concatenate([jnp.sin(cc), jnp.cos(cc)], axis=-1)
    emb = jnp.concatenate([jnp.broadcast_to(r_emb[:, None, :], (rows, cols, dim // 2)),
                           jnp.broadcast_to(c_emb[None, :, :], (rows, cols, dim // 2))], axis=-1)
    return emb.reshape(rows * cols, dim)


def _pad_lanes(a):
    return jnp.pad(a, ((0, 0), (0, LANES - a.shape[1])))


def kernel(x_prompt, x_sample, c, state_delta, c_ctx, norm1_w, norm2_w, w_mod, b_mod, w_in, w_out, qkv_conv_w, delta_a_log, delta_dt_bias, delta_norm_w, sgu_norm_w, sgu_w, sgu_b, sconv_w, pool_w, pool_scale, ffn_w_gate, ffn_w_up, ffn_w_down, router_w, moe_w_gate, moe_w_up, moe_w_down, final_norm_w):
    bc, sc, d = x_prompt.shape
    bd, sd, _ = x_sample.shape
    depth = w_in.shape[0]
    mc, ml = bc * sc, bd * sd
    m = mc + ml
    w_a = H_A * LANES
    w_b, w_c, w_d = sgu_norm_w.shape[1], sconv_w.shape[1], pool_scale.shape[1]
    assert d == ROW_CHUNKS * LANES and mc % sd == 0 and w_a == w_b == w_c == w_d == d // 4
    assert delta_a_log.shape[1:] == (2, H_A) and delta_norm_w.shape[1] == LANES

    tm_mm = 512 if m % 512 == 0 else 256
    grid_w = 64
    pos = _sincos_2d(sd // grid_w, grid_w, d)
    x = jnp.concatenate([x_prompt.reshape(mc, d), (x_sample + pos[None]).reshape(ml, d)], axis=0)

    rm = -(-(1 + bd) // 8) * 8
    cond = jnp.zeros((rm, d), F32).at[0].set(c_ctx).at[1:1 + bd].set(c)
    mod_all = _mod_call(cond, w_mod, b_mod).reshape(depth, rm, 6, d)

    o0 = 3 * w_a
    o1 = o0 + w_a
    o3 = o1 + 4 * H_A
    o4 = o3 + 2 * w_b
    o5 = o4 + 3 * w_c
    blk0 = mc // sd
    ctx_states = []
    for l in range(depth):
        mod = mod_all[l]
        w_main = jnp.concatenate([w_in[l][:, :o1], w_in[l][:, o3:]], axis=1).astype(BF16)
        w_ab = _pad_lanes(w_in[l][:, o1:o3]).astype(BF16)
        proj, ab = _inproj_call(x, norm1_w[l][None], mod, w_main, w_ab, mc, sd, tm_mm, 1024)

        conv_t = qkv_conv_w[l].T
        alog = _pad_lanes(delta_a_log[l].reshape(1, 2 * H_A))
        dtb = _pad_lanes(delta_dt_bias[l].reshape(1, 2 * H_A))
        nwa = delta_norm_w[l][None]
        mix_a, s_ctx = _delta_call(proj, ab, conv_t, alog, dtb, nwa, None, l, None,
                                   t_len=sc, nseq=bc, blk0=0, write_state=True)
        (mix_a,) = _delta_call(proj, ab, conv_t, alog, dtb, nwa, state_delta, l, mix_a,
                               t_len=sd, nseq=bd, blk0=blk0, write_state=False)
        ctx_states.append(s_ctx)

        def both_paths(kern, name, col_block, width_in, params, pspecs, width_out):
            out = _seq_call(kern, name, proj, col_block, width_in, params, pspecs, None,
                            t_len=sc, nseq=bc, blk0=0, width_out=width_out)
            return _seq_call(kern, name, proj, col_block, width_in, params, pspecs, out,
                             t_len=sd, nseq=bd, blk0=blk0, width_out=width_out)

        n_g = sgu_w.shape[1]
        mix_b = both_paths(
            _sgu_kernel, "sgu", o1 // (2 * w_b), 2 * w_b,
            [sgu_norm_w[l][None], sgu_w[l].astype(BF16), sgu_b[l].reshape(n_g, SGU_CHUNK, 1)],
            [pl.BlockSpec((1, w_b), lambda b: (0, 0)),
             pl.BlockSpec((n_g, SGU_CHUNK, SGU_CHUNK), lambda b: (0, 0, 0)),
             pl.BlockSpec((n_g, SGU_CHUNK, 1), lambda b: (0, 0, 0))], w_b)
        mix_c = both_paths(
            _sconv_kernel, "sconv", (o1 + 2 * w_b) // (3 * w_c), 3 * w_c,
            [sconv_w[l].T], [pl.BlockSpec((3, w_c), lambda b: (0, 0))], w_c)
        n_gd = pool_w.shape[1]
        mix_d = both_paths(
            _pool_kernel, "pool", (o1 + 2 * w_b + 3 * w_c) // w_d, w_d,
            [pool_w[l].astype(BF16), pool_scale[l][None]],
            [pl.BlockSpec((n_gd, LANES, LANES), lambda b: (0, 0, 0)),
             pl.BlockSpec((1, w_d), lambda b: (0, 0))], w_d)

        x = _outproj_call(x, (mix_a, mix_b, mix_c, mix_d), w_out[l].astype(BF16), mod, mc, sd, tm_mm)

        jl = l // 2
        if l % 2 == 0:
            x = _ffn_call(x, norm2_w[l][None], mod, ffn_w_gate[jl].astype(BF16), ffn_w_up[jl].astype(BF16),
                          ffn_w_down[jl].astype(BF16), mc, sd, tm_mm, 512)
        else:
            x = _moe_layer(x, norm2_w[l][None], mod, _pad_lanes(router_w[jl]),
                           moe_w_gate[jl].astype(BF16), moe_w_up[jl].astype(BF16), moe_w_down[jl].astype(BF16),
                           mc, sd, tm_route=256, tm=512, tf=512, tc=256, gather_rows=1024)

    y = _final_call(x, final_norm_w[None], tm_mm)
    y_prompt = y[:mc].reshape(bc, sc, d)
    y_sample = y[mc:].reshape(bd, sd, d)
    new_state = jnp.stack(ctx_states, axis=1).astype(x_prompt.dtype)
    return (y_prompt, y_sample, new_state)
```

```python
import functools

import jax
import jax.numpy as jnp
from jax import lax
from jax.experimental import pallas as pl
from jax.experimental.pallas import tpu as pltpu

F32 = jnp.float32
BF16 = jnp.bfloat16

LANES = 128
ROW_CHUNKS = 16
DELTA_CHUNK = 64
SGU_CHUNK = 128
H_A = 4
POOL_WINDOWS = (2, 4, 8, 16)
TOP_K = 2
VMEM_LIMIT = 56 * 1024 * 1024


def _cparams(*sem):
    return pltpu.CompilerParams(dimension_semantics=sem, vmem_limit_bytes=VMEM_LIMIT)


def _silu(x):
    return x / (1.0 + jnp.exp(-x))


def _sigmoid(x):
    return 1.0 / (1.0 + jnp.exp(-x))


def _softplus(x):
    return jnp.maximum(x, 0.0) + jnp.log1p(jnp.exp(-jnp.abs(x)))


def _gelu_tanh(x):
    return 0.5 * x * (1.0 + jnp.tanh(0.7978845608028654 * (x + 0.044715 * (x * x * x))))


def _rms(x, w, eps=1e-6):
    return x * lax.rsqrt(jnp.mean(x * x, axis=-1, keepdims=True) + eps) * w


def _dot(a, b):
    return jnp.dot(a, b, preferred_element_type=F32)


def _split(a):
    hi = a.astype(BF16)
    lo = (a - hi.astype(F32)).astype(BF16)
    return hi, lo


def _dot3(a, b):
    a_hi, a_lo = _split(a)
    b_hi, b_lo = _split(b)
    return _dot(a_hi, b_hi) + _dot(a_hi, b_lo) + _dot(a_lo, b_hi)


def _bmm(a, b):
    return jnp.einsum('nij,njk->nik', a, b, preferred_element_type=F32)


def _bmm16(a, b):
    return _bmm(a.astype(BF16), b.astype(BF16))


def _mod_kernel(c_ref, w_ref, b_ref, o_ref):
    a = _silu(c_ref[...]).astype(BF16)
    o_ref[...] = _dot(a, w_ref[...].astype(BF16)) + b_ref[...]


def _mod_call(cond, w_mod, b_mod, tn=1024):
    depth, d, n = w_mod.shape
    rm = cond.shape[0]
    return pl.pallas_call(
        _mod_kernel,
        grid=(depth, n // tn),
        in_specs=[pl.BlockSpec((rm, d), lambda l, j: (0, 0)),
                  pl.BlockSpec((None, d, tn), lambda l, j: (l, 0, j)),
                  pl.BlockSpec((None, 1, tn), lambda l, j: (l, 0, j))],
        out_specs=pl.BlockSpec((None, rm, tn), lambda l, j: (l, 0, j)),
        out_shape=jax.ShapeDtypeStruct((depth, rm, n), F32),
        compiler_params=_cparams("parallel", "parallel"),
        name="adaln_mod",
    )(cond, w_mod, b_mod.reshape(depth, 1, n))


def _mod_row_index(i, tm, mc, sd):
    r0 = i * tm
    return jnp.where(r0 < mc, 0, 1 + (r0 - mc) // sd)


def _inproj_kernel(x_ref, nw_ref, mod_ref, w_ref, wab_ref, proj_ref, ab_ref, h_ref):
    @pl.when(pl.program_id(1) == 0)
    def _():
        h = _rms(x_ref[...], nw_ref[...]) * (1.0 + mod_ref[1:2, :]) + mod_ref[0:1, :]
        hb = h.astype(BF16)
        h_ref[...] = hb
        ab_ref[...] = _dot(hb, wab_ref[...])

    proj_ref[...] = _dot(h_ref[...], w_ref[...])


def _inproj_call(x, nw, mod, w_main, w_ab, layer, mc, sd, tm, tn):
    m, d = x.shape
    n = w_main.shape[2]
    midx = functools.partial(_mod_row_index, tm=tm, mc=mc, sd=sd)
    return pl.pallas_call(
        _inproj_kernel,
        grid=(m // tm, n // tn),
        in_specs=[pl.BlockSpec((tm, d), lambda i, j: (i, 0)),
                  pl.BlockSpec((1, d), lambda i, j: (0, 0)),
                  pl.BlockSpec((None, 6, d), lambda i, j: (midx(i), 0, 0)),
                  pl.BlockSpec((None, d, tn), lambda i, j: (layer, 0, j)),
                  pl.BlockSpec((None, d, LANES), lambda i, j: (layer, 0, 0))],
        out_specs=[pl.BlockSpec((tm, tn), lambda i, j: (i, j)),
                   pl.BlockSpec((tm, LANES), lambda i, j: (i, 0))],
        out_shape=[jax.ShapeDtypeStruct((m, n), F32), jax.ShapeDtypeStruct((m, LANES), F32)],
        scratch_shapes=[pltpu.VMEM((tm, d), BF16)],
        compiler_params=_cparams("parallel", "arbitrary"),
        name="in_proj",
    )(x, nw, mod, w_main, w_ab)


def _shift_rows(x, d, row, t_len):
    if d == 0:
        return x
    y = pltpu.roll(x, (-d) % t_len, axis=0)
    ok = (row + d >= 0) & (row + d < t_len)
    return jnp.where(ok, y, 0.0)


def _conv3(x, w, row, t_len):
    return (_shift_rows(x, -1, row, t_len) * w[0:1, :] + x * w[1:2, :]
            + _shift_rows(x, 1, row, t_len) * w[2:3, :])


def _chunk_cumsum(g, row, t_len, reverse):
    pos = row % DELTA_CHUNK
    s = 1
    while s < DELTA_CHUNK:
        if reverse:
            g = g + jnp.where(pos < DELTA_CHUNK - s, pltpu.roll(g, t_len - s, axis=0), 0.0)
        else:
            g = g + jnp.where(pos >= s, pltpu.roll(g, s, axis=0), 0.0)
        s *= 2
    return g


def _unit_tri_inverse(lmat, r, c):
    eye = jnp.where(r == c, 1.0, 0.0)

    def same_block(b):
        return (r // b) == (c // b)

    x = jnp.where(same_block(8), -lmat, 0.0)
    x2 = _bmm16(x, x)
    x4 = _bmm16(x2, x2)
    p = _bmm16(_bmm16(eye + x, eye + x2), eye + x4)
    b = 8
    while b < DELTA_CHUNK:
        off = jnp.where(same_block(2 * b) & jnp.logical_not(same_block(b)), lmat, 0.0)
        pb = p.astype(BF16)
        p = p - _bmm16(_bmm16(pb, off), pb)
        b *= 2
    return p


def _delta_kernel(*refs, t_len, hps, has_s0, write_state):
    (q_ref, k_ref, v_ref, ga_ref, ab_ref, cq_ref, ck_ref, cv_ref, alog_ref, dtb_ref, nw_ref), rest = refs[:11], refs[11:]
    if has_s0:
        s0_ref, rest = rest[0], rest[1:]
    o_ref, rest = rest[0], rest[1:]
    if write_state:
        sout_ref = rest[0]

    n_ch = t_len // DELTA_CHUNK
    cl = DELTA_CHUNK
    row = lax.broadcasted_iota(jnp.int32, (t_len, LANES), 0)
    lane = lax.broadcasted_iota(jnp.int32, (t_len, LANES), 1)
    r64 = lax.broadcasted_iota(jnp.int32, (cl, cl), 0)
    c64 = lax.broadcasted_iota(jnp.int32, (cl, cl), 1)

    def l2n(x):
        return x * lax.rsqrt(jnp.sum(x * x, axis=-1, keepdims=True) + 1e-6)

    def column(a, idx):
        col = jnp.sum(jnp.where(lane == idx, a, 0.0), axis=1, keepdims=True)
        return jnp.broadcast_to(col, (t_len, LANES))

    def chunks(a):
        return a.reshape(n_ch, cl, LANES)

    ab = ab_ref[...]
    g_all = -jnp.exp(alog_ref[...]) * _softplus(ab + dtb_ref[...])
    b_all = _sigmoid(ab)

    chains = []
    for hh in range(hps):
        head = pl.program_id(1) * hps + hh
        cs = slice(hh * LANES, (hh + 1) * LANES)
        q = l2n(_silu(_conv3(q_ref[:, cs], cq_ref[:, cs], row, t_len))) * (LANES ** -0.5)
        k = l2n(_silu(_conv3(k_ref[:, cs], ck_ref[:, cs], row, t_len)))
        v = _silu(_conv3(v_ref[:, cs], cv_ref[:, cs], row, t_len))
        k3 = chunks(k)
        kb16 = k3.astype(BF16)
        kk = jnp.einsum('ncd,nsd->ncs', kb16, kb16, preferred_element_type=F32)
        qk = jnp.einsum('ncd,nsd->ncs', chunks(q).astype(BF16), kb16, preferred_element_type=F32)
        for direction in range(2):
            rev = direction == 1
            g = column(g_all, direction * H_A + head)
            beta = column(b_all, 2 * H_A + direction * H_A + head)
            gc = _chunk_cumsum(g, row, t_len, rev)
            gc3 = chunks(gc)
            gc_rows = jnp.swapaxes(gc3, 1, 2)[:, :cl, :]
            dmat = gc3[:, :, :cl] - gc_rows
            if rev:
                incl, strict = c64 >= r64, c64 > r64
            else:
                incl, strict = c64 <= r64, c64 < r64
            decay = jnp.exp(jnp.where(incl, dmat, 0.0))
            lmat = jnp.where(strict, chunks(beta)[:, :, :cl] * kk * decay, 0.0)
            tinv = _unit_tri_inverse(lmat, r64, c64).astype(BF16)
            egc = jnp.exp(gc)
            last = 0 if rev else cl - 1
            glast = gc3[:, last:last + 1, :]
            if has_s0:
                s_init = s0_ref[direction, hh]
            else:
                s_init = jnp.zeros((LANES, LANES), F32)
            chains.append(dict(
                hh=hh, direction=direction, s=s_init, outs=[None] * n_ch,
                order=list(range(n_ch - 1, -1, -1) if rev else range(n_ch)),
                attn=jnp.where(incl, qk * decay, 0.0).astype(BF16),
                u=_bmm16(tinv, chunks(v * beta)),
                w=_bmm16(tinv, chunks(k * beta * egc)).astype(BF16),
                qg=chunks(q * egc).astype(BF16),
                kdt=jnp.swapaxes(k3 * jnp.exp(glast - gc3), 1, 2).astype(BF16),
                gl=jnp.exp(glast)))

    for t in range(n_ch):
        for ch in chains:
            n = ch['order'][t]
            sb = ch['s'].astype(BF16)
            v_new = ch['u'][n] - _dot(ch['w'][n], sb)
            vb = v_new.astype(BF16)
            ch['outs'][n] = _dot(ch['qg'][n], sb) + _dot(ch['attn'][n], vb)
            ch['s'] = ch['s'] * ch['gl'][n] + _dot(ch['kdt'][n], vb)

    for ch in chains:
        if write_state:
            sout_ref[ch['direction'], ch['hh']] = ch['s']
    for hh in range(hps):
        cs = slice(hh * LANES, (hh + 1) * LANES)
        fwd, bwd = chains[2 * hh], chains[2 * hh + 1]
        o = jnp.concatenate([a + b for a, b in zip(fwd['outs'], bwd['outs'])], axis=0)
        o_ref[:, cs] = (_rms(o, nw_ref[...]) * _silu(ga_ref[:, cs])).astype(o_ref.dtype)


def _delta_call(proj, ab, conv_t, alog, dtb, nw, s0, layer, *, t_len, nseq, blk0, write_state, hps=2):
    has_s0 = s0 is not None
    w_a = H_A * LANES
    wh = hps * LANES
    nhb = H_A // hps

    def pspec(part):
        return pl.BlockSpec((t_len, wh), lambda b, h: (blk0 + b, part * nhb + h))

    def cspec(part):
        return pl.BlockSpec((3, wh), lambda b, h: (0, part * nhb + h))

    row1 = pl.BlockSpec((1, LANES), lambda b, h: (0, 0))
    in_specs = [pspec(0), pspec(1), pspec(2), pspec(3),
                pl.BlockSpec((t_len, LANES), lambda b, h: (blk0 + b, 0)),
                cspec(0), cspec(1), cspec(2), row1, row1, row1]
    args = [proj, proj, proj, proj, ab, conv_t, conv_t, conv_t, alog, dtb, nw]
    if has_s0:
        in_specs.append(pl.BlockSpec((None, None, 2, hps, LANES, LANES), lambda b, h: (b, layer, 0, h, 0, 0)))
        args.append(s0)
    out_specs = [pl.BlockSpec((t_len, wh), lambda b, h: (b, h))]
    out_shape = [jax.ShapeDtypeStruct((nseq * t_len, w_a), BF16)]
    if write_state:
        out_specs.append(pl.BlockSpec((None, 2, hps, LANES, LANES), lambda b, h: (b, 0, h, 0, 0)))
        out_shape.append(jax.ShapeDtypeStruct((nseq, 2, H_A, LANES, LANES), F32))

    return pl.pallas_call(
        functools.partial(_delta_kernel, t_len=t_len, hps=hps, has_s0=has_s0, write_state=write_state),
        grid=(nseq, nhb),
        in_specs=in_specs,
        out_specs=out_specs,
        out_shape=out_shape,
        compiler_params=_cparams("parallel", "parallel"),
        name=f"delta_T{t_len}",
    )(*args)


def _sgu_kernel(z_ref, nw_ref, w_ref, b_ref, o_ref, *, t_len):
    wb = o_ref.shape[1]
    z = _gelu_tanh(z_ref[...])
    u, v = z[:, :wb], z[:, wb:]
    mu = jnp.mean(v, axis=-1, keepdims=True)
    vc = v - mu
    vn = (vc * lax.rsqrt(jnp.mean(vc * vc, axis=-1, keepdims=True) + 1e-5) * nw_ref[...]).astype(BF16)
    for n in range(t_len // SGU_CHUNK):
        rs = slice(n * SGU_CHUNK, (n + 1) * SGU_CHUNK)
        for g in range(wb // LANES):
            cs = slice(g * LANES, (g + 1) * LANES)
            sp = _dot(w_ref[g], vn[rs, cs]) + b_ref[g]
            o_ref[rs, cs] = (u[rs, cs] * sp).astype(o_ref.dtype)


def _sconv_kernel(c_ref, w_ref, o_ref, *, t_len):
    wc = o_ref.shape[1]
    row = lax.broadcasted_iota(jnp.int32, (t_len, wc), 0)
    x = c_ref[...]
    y = _conv3(x[:, wc:2 * wc] * x[:, 2 * wc:], w_ref[...], row, t_len)
    o_ref[...] = (x[:, :wc] * y).astype(o_ref.dtype)


def _pool_kernel(p_ref, w_ref, sc_ref, o_ref, *, t_len):
    row = lax.broadcasted_iota(jnp.int32, (t_len, LANES), 0)
    for j, win in enumerate(POOL_WINDOWS):
        cs = slice(j * LANES, (j + 1) * LANES)
        x = p_ref[:, cs]
        half = win // 2
        acc = x
        for d in range(-half, win - half):
            if d != 0:
                acc = acc + _shift_rows(x, d, row, t_len)
        cnt = (jnp.minimum(row + (win - half), t_len) - jnp.maximum(row - half, 0)).astype(F32)
        pooled = acc / cnt - x
        o_ref[:, cs] = (_dot(pooled.astype(BF16), w_ref[j]) * sc_ref[:, cs]).astype(o_ref.dtype)


def _seq_call(kern, name, proj, col_block, width_in, params, pspecs, *, t_len, nseq, blk0, width_out):
    return pl.pallas_call(
        functools.partial(kern, t_len=t_len),
        grid=(nseq,),
        in_specs=[pl.BlockSpec((t_len, width_in), lambda b: (blk0 + b, col_block))] + pspecs,
        out_specs=pl.BlockSpec((t_len, width_out), lambda b: (b, 0)),
        out_shape=jax.ShapeDtypeStruct((nseq * t_len, width_out), BF16),
        compiler_params=_cparams("parallel"),
        name=f"{name}_T{t_len}",
    )(proj, *params)


def _outproj_kernel(x_ref, *refs, n_ctx_tiles):
    n_mix = (len(refs) - 3) // 2
    ctx_refs, lat_refs = refs[:n_mix], refs[n_mix:2 * n_mix]
    w_ref, mod_ref, o_ref = refs[2 * n_mix:]

    def run(mix_refs):
        acc = None
        for i, m_ref in enumerate(mix_refs):
            wq = m_ref.shape[1]
            part = _dot(m_ref[...], w_ref[i * wq:(i + 1) * wq, :])
            acc = part if acc is None else acc + part
        o_ref[...] = x_ref[...] + mod_ref[2:3, :] * acc

    is_ctx = pl.program_id(0) < n_ctx_tiles
    pl.when(is_ctx)(lambda: run(ctx_refs))
    pl.when(jnp.logical_not(is_ctx))(lambda: run(lat_refs))


def _outproj_call(x, mixes_ctx, mixes_lat, w_out, layer, mod, mc, sd, tm):
    m, d = x.shape
    wq = mixes_ctx[0].shape[1]
    nct = mc // tm
    midx = functools.partial(_mod_row_index, tm=tm, mc=mc, sd=sd)
    cspec = pl.BlockSpec((tm, wq), lambda i: (jnp.minimum(i, nct - 1), 0))
    lspec = pl.BlockSpec((tm, wq), lambda i: (jnp.maximum(i - nct, 0), 0))
    n_mix = len(mixes_ctx)
    return pl.pallas_call(
        functools.partial(_outproj_kernel, n_ctx_tiles=nct),
        grid=(m // tm,),
        in_specs=[pl.BlockSpec((tm, d), lambda i: (i, 0))] + [cspec] * n_mix + [lspec] * n_mix
                 + [pl.BlockSpec((None, d, d), lambda i: (layer, 0, 0)),
                    pl.BlockSpec((None, 6, d), lambda i: (midx(i), 0, 0))],
        out_specs=pl.BlockSpec((tm, d), lambda i: (i, 0)),
        out_shape=jax.ShapeDtypeStruct((m, d), F32),
        compiler_params=_cparams("parallel"),
        name="out_proj",
    )(x, *mixes_ctx, *mixes_lat, w_out, mod)


def _ffn_kernel(x_ref, nw_ref, mod_ref, wg_ref, wu_ref, wd_ref, o_ref, h_ref, acc_ref):
    j = pl.program_id(1)

    @pl.when(j == 0)
    def _():
        h = _rms(x_ref[...], nw_ref[...]) * (1.0 + mod_ref[4:5, :]) + mod_ref[3:4, :]
        h_ref[...] = h.astype(BF16)
        acc_ref[...] = jnp.zeros_like(acc_ref)

    h = h_ref[...]
    a = (_silu(_dot(h, wg_ref[...])) * _dot(h, wu_ref[...])).astype(BF16)
    acc_ref[...] += _dot(a, wd_ref[...])

    @pl.when(j == pl.num_programs(1) - 1)
    def _():
        o_ref[...] = x_ref[...] + mod_ref[5:6, :] * acc_ref[...]


def _ffn_call(x, nw, mod, wg, wu, wd, layer, mc, sd, tm, tf):
    m, d = x.shape
    f = wg.shape[2]
    midx = functools.partial(_mod_row_index, tm=tm, mc=mc, sd=sd)
    return pl.pallas_call(
        _ffn_kernel,
        grid=(m // tm, f // tf),
        in_specs=[pl.BlockSpec((tm, d), lambda i, j: (i, 0)),
                  pl.BlockSpec((1, d), lambda i, j: (0, 0)),
                  pl.BlockSpec((None, 6, d), lambda i, j: (midx(i), 0, 0)),
                  pl.BlockSpec((None, d, tf), lambda i, j: (layer, 0, j)),
                  pl.BlockSpec((None, d, tf), lambda i, j: (layer, 0, j)),
                  pl.BlockSpec((None, tf, d), lambda i, j: (layer, j, 0))],
        out_specs=pl.BlockSpec((tm, d), lambda i, j: (i, 0)),
        out_shape=jax.ShapeDtypeStruct((m, d), F32),
        scratch_shapes=[pltpu.VMEM((tm, d), BF16), pltpu.VMEM((tm, d), F32)],
        compiler_params=_cparams("parallel", "arbitrary"),
        name="ffn_dense",
    )(x, nw, mod, wg, wu, wd)


def _route_kernel(x_ref, nw_ref, mod_ref, rw_ref, h_ref, r_ref, *, n_experts):
    tm = x_ref.shape[0]
    h = _rms(x_ref[...], nw_ref[...]) * (1.0 + mod_ref[4:5, :]) + mod_ref[3:4, :]
    for cb in range(ROW_CHUNKS):
        h_ref[pl.ds(cb, tm, stride=ROW_CHUNKS), :] = h[:, cb * LANES:(cb + 1) * LANES]
    lane = lax.broadcasted_iota(jnp.int32, (tm, LANES), 1).astype(F32)
    neg = jnp.float32(-jnp.inf)
    logits = jnp.where(lane < n_experts, _dot3(h, rw_ref[...]), neg)
    m1 = jnp.max(logits, axis=-1, keepdims=True)
    i1 = jnp.min(jnp.where(logits == m1, lane, float(LANES)), axis=-1, keepdims=True)
    rest = jnp.where(lane == i1, neg, logits)
    m2 = jnp.max(rest, axis=-1, keepdims=True)
    i2 = jnp.min(jnp.where(rest == m2, lane, float(LANES)), axis=-1, keepdims=True)
    e2 = jnp.exp(m2 - m1)
    den = 1.0 + e2
    r_ref[...] = jnp.where(lane == 0.0, i1,
                           jnp.where(lane == 1.0, i2,
                                     jnp.where(lane == 2.0, 1.0 / den, jnp.where(lane == 3.0, e2 / den, 0.0))))


def _route_call(x, nw, mod, rw, mc, sd, tm, n_experts):
    m, d = x.shape
    midx = functools.partial(_mod_row_index, tm=tm, mc=mc, sd=sd)
    return pl.pallas_call(
        functools.partial(_route_kernel, n_experts=n_experts),
        grid=(m // tm,),
        in_specs=[pl.BlockSpec((tm, d), lambda i: (i, 0)),
                  pl.BlockSpec((1, d), lambda i: (0, 0)),
                  pl.BlockSpec((None, 6, d), lambda i: (midx(i), 0, 0)),
                  pl.BlockSpec((d, LANES), lambda i: (0, 0))],
        out_specs=[pl.BlockSpec((tm * ROW_CHUNKS, LANES), lambda i: (i, 0)),
                   pl.BlockSpec((tm, LANES), lambda i: (i, 0))],
        out_shape=[jax.ShapeDtypeStruct((m * ROW_CHUNKS, LANES), F32),
                   jax.ShapeDtypeStruct((m, LANES), F32)],
        compiler_params=_cparams("parallel"),
        name="moe_route",
    )(x, nw, mod, rw)


def _row_copy(src_hbm, dst_ref, src_row, dst_row, sem):
    s0 = pl.multiple_of(src_row * ROW_CHUNKS, ROW_CHUNKS)
    d0 = pl.multiple_of(dst_row * ROW_CHUNKS, ROW_CHUNKS)
    return pltpu.make_async_copy(src_hbm.at[pl.ds(s0, ROW_CHUNKS), :], dst_ref.at[pl.ds(d0, ROW_CHUNKS), :], sem)


def _gather_kernel(nrows_ref, idx_ref, src_hbm, o_ref, sem, *, rows):
    used = pl.program_id(0) * rows < nrows_ref[0]

    @pl.when(used)
    def _():
        def start(r, carry):
            _row_copy(src_hbm, o_ref, idx_ref[0, r], r, sem).start()
            return carry

        def wait(r, carry):
            _row_copy(src_hbm, o_ref, idx_ref[0, r], r, sem).wait()
            return carry

        lax.fori_loop(0, rows, start, 0)
        lax.fori_loop(0, rows, wait, 0)

    @pl.when(jnp.logical_not(used))
    def _():
        o_ref[...] = jnp.zeros_like(o_ref)


def _gather_call(h_rows, src_idx, nrows, rows):
    p = src_idx.shape[0]
    return pl.pallas_call(
        functools.partial(_gather_kernel, rows=rows),
        grid_spec=pltpu.PrefetchScalarGridSpec(
            num_scalar_prefetch=1,
            grid=(p // rows,),
            in_specs=[pl.BlockSpec((None, 1, rows), lambda i, n: (i, 0, 0), memory_space=pltpu.SMEM),
                      pl.BlockSpec(memory_space=pl.ANY)],
            out_specs=pl.BlockSpec((rows * ROW_CHUNKS, LANES), lambda i, n: (i, 0)),
            scratch_shapes=[pltpu.SemaphoreType.DMA(())]),
        out_shape=jax.ShapeDtypeStruct((p * ROW_CHUNKS, LANES), F32),
        compiler_params=_cparams("arbitrary"),
        name="moe_gather",
    )(nrows, src_idx.reshape(p // rows, 1, rows), h_rows)


def _experts_kernel(te_ref, nt_ref, xs_ref, wg_ref, wu_ref, wd_ref, o_ref, x_ref, acc_ref):
    i, j = pl.program_id(0), pl.program_id(1)
    tm = x_ref.shape[0]

    @pl.when(i < nt_ref[0])
    def _():
        @pl.when(j == 0)
        def _():
            for cb in range(ROW_CHUNKS):
                x_ref[:, cb * LANES:(cb + 1) * LANES] = xs_ref[pl.ds(cb, tm, stride=ROW_CHUNKS), :].astype(BF16)
            acc_ref[...] = jnp.zeros_like(acc_ref)

        x = x_ref[...]
        a = (_silu(_dot(x, wg_ref[...])) * _dot(x, wu_ref[...])).astype(BF16)
        acc_ref[...] += _dot(a, wd_ref[...])

        @pl.when(j == pl.num_programs(1) - 1)
        def _():
            for cb in range(ROW_CHUNKS):
                o_ref[pl.ds(cb, tm, stride=ROW_CHUNKS), :] = acc_ref[:, cb * LANES:(cb + 1) * LANES]

    @pl.when((i >= nt_ref[0]) & (j == pl.num_programs(1) - 1))
    def _():
        o_ref[...] = jnp.zeros_like(o_ref)


def _experts_call(tile_e, ntiles, xs, wg, wu, wd, layer, tm, tf):
    d, f = wg.shape[2:]
    p = xs.shape[0] // ROW_CHUNKS
    nf = f // tf

    def row_blk(i, j, te, nt):
        return (jnp.minimum(i, nt[0] - 1), 0)

    def jj(i, j, nt):
        return jnp.where(i < nt[0], j, nf - 1)

    def w_in_blk(i, j, te, nt):
        return (layer, te[i], 0, jj(i, j, nt))

    def w_out_blk(i, j, te, nt):
        return (layer, te[i], jj(i, j, nt), 0)

    return pl.pallas_call(
        _experts_kernel,
        grid_spec=pltpu.PrefetchScalarGridSpec(
            num_scalar_prefetch=2,
            grid=(p // tm, nf),
            in_specs=[pl.BlockSpec((tm * ROW_CHUNKS, LANES), row_blk),
                      pl.BlockSpec((None, None, d, tf), w_in_blk),
                      pl.BlockSpec((None, None, d, tf), w_in_blk),
                      pl.BlockSpec((None, None, tf, d), w_out_blk)],
            out_specs=pl.BlockSpec((tm * ROW_CHUNKS, LANES), lambda i, j, te, nt: (i, 0)),
            scratch_shapes=[pltpu.VMEM((tm, d), BF16), pltpu.VMEM((tm, d), F32)]),
        out_shape=jax.ShapeDtypeStruct((p * ROW_CHUNKS, LANES), F32),
        compiler_params=_cparams("arbitrary", "arbitrary"),
        name="moe_experts",
    )(tile_e, ntiles, xs, wg, wu, wd)


def _combine_kernel(idx_ref, x_ref, r_ref, mod_ref, ys_hbm, o_ref, buf_ref, sem):
    tc = x_ref.shape[0]
    n_rows = TOP_K * tc

    def start(r, carry):
        _row_copy(ys_hbm, buf_ref, idx_ref[0, r], r, sem).start()
        return carry

    def wait(r, carry):
        _row_copy(ys_hbm, buf_ref, idx_ref[0, r], r, sem).wait()
        return carry

    lax.fori_loop(0, n_rows, start, 0)
    lax.fori_loop(0, n_rows, wait, 0)
    w1 = r_ref[:, 2:3]
    w2 = r_ref[:, 3:4]
    for cb in range(ROW_CHUNKS):
        cs = slice(cb * LANES, (cb + 1) * LANES)
        y1 = buf_ref[pl.ds(cb, tc, stride=ROW_CHUNKS), :]
        y2 = buf_ref[pl.ds(tc * ROW_CHUNKS + cb, tc, stride=ROW_CHUNKS), :]
        o_ref[:, cs] = x_ref[:, cs] + mod_ref[5:6, cs] * (y1 * w1 + y2 * w2)


def _combine_call(dest, x, route, mod, ys, mc, sd, tc):
    m, d = x.shape
    midx = functools.partial(_mod_row_index, tm=tc, mc=mc, sd=sd)
    return pl.pallas_call(
        _combine_kernel,
        grid=(m // tc,),
        in_specs=[pl.BlockSpec((None, 1, TOP_K * tc), lambda i: (i, 0, 0), memory_space=pltpu.SMEM),
                  pl.BlockSpec((tc, d), lambda i: (i, 0)),
                  pl.BlockSpec((tc, LANES), lambda i: (i, 0)),
                  pl.BlockSpec((None, 6, d), lambda i: (midx(i), 0, 0)),
                  pl.BlockSpec(memory_space=pl.ANY)],
        out_specs=pl.BlockSpec((tc, d), lambda i: (i, 0)),
        out_shape=jax.ShapeDtypeStruct((m, d), F32),
        scratch_shapes=[pltpu.VMEM((TOP_K * tc * ROW_CHUNKS, LANES), F32), pltpu.SemaphoreType.DMA(())],
        compiler_params=_cparams("arbitrary"),
        name="moe_combine",
    )(dest, x, route, mod, ys)


def _moe_plan(route, n_experts, tm, n_tiles, tc):
    m = route.shape[0]
    e_flat = jnp.concatenate([route[:, 0], route[:, 1]]).astype(jnp.int32)
    onehot = (e_flat[:, None] == jnp.arange(n_experts, dtype=jnp.int32)[None, :]).astype(jnp.int32)
    csum = jnp.cumsum(onehot, axis=0)
    cnt = csum[-1]
    rank = jnp.take_along_axis(csum, e_flat[:, None], axis=1)[:, 0] - 1
    gsz = ((cnt + tm - 1) // tm) * tm
    off_end = jnp.cumsum(gsz)
    dest = (off_end - gsz)[e_flat] + rank
    n_rows = off_end[-1]
    n_used = n_rows // tm
    tiles = jnp.arange(n_tiles, dtype=jnp.int32)
    tile_e = jnp.sum((off_end[None, :] <= (tiles * tm)[:, None]).astype(jnp.int32), axis=1)
    tile_e = jnp.minimum(tile_e, n_experts - 1)
    tile_e = jnp.where(tiles < n_used, tile_e, tile_e[jnp.maximum(n_used - 1, 0)])
    tok = jnp.tile(jnp.arange(m, dtype=jnp.int32), TOP_K)
    src = jnp.zeros((n_tiles * tm,), jnp.int32).at[dest].set(tok)
    dest_tiles = jnp.concatenate([dest[:m].reshape(m // tc, 1, tc), dest[m:].reshape(m // tc, 1, tc)], axis=2)
    return src, dest_tiles, tile_e, n_used.reshape(1).astype(jnp.int32), n_rows.reshape(1).astype(jnp.int32)


def _moe_layer(x, nw, mod, rw, wg, wu, wd, layer, mc, sd, tm_route, tm, tf, tc, gather_rows):
    m, d = x.shape
    n_experts = wg.shape[1]
    p_rows = TOP_K * m + n_experts * tm
    p_rows = -(-p_rows // gather_rows) * gather_rows
    n_tiles = p_rows // tm
    h_rows, route = _route_call(x, nw, mod, rw, mc, sd, tm_route, n_experts)
    src, dest_tiles, tile_e, n_used, n_rows = _moe_plan(route, n_experts, tm, n_tiles, tc)
    xs = _gather_call(h_rows, src, n_rows, gather_rows)
    ys = _experts_call(tile_e, n_used, xs, wg, wu, wd, layer, tm, tf)
    return _combine_call(dest_tiles, x, route, mod, ys, mc, sd, tc)


def _final_kernel(x_ref, w_ref, o_ref):
    o_ref[...] = _rms(x_ref[...], w_ref[...])


def _final_call(x, w, tm):
    m, d = x.shape
    return pl.pallas_call(
        _final_kernel,
        grid=(m // tm,),
        in_specs=[pl.BlockSpec((tm, d), lambda i: (i, 0)), pl.BlockSpec((1, d), lambda i: (0, 0))],
        out_specs=pl.BlockSpec((tm, d), lambda i: (i, 0)),
        out_shape=jax.ShapeDtypeStruct((m, d), F32),
        compiler_params=_cparams("parallel"),
        name="final_norm",
    )(x, w)


def _sincos_2d(rows, cols, dim):
    quarter = dim // 4
    omega = 1.0 / (10000.0 ** (jnp.arange(quarter, dtype=F32) / quarter))
    r = jnp.arange(rows, dtype=F32)[:, None] * omega
    cc = jnp.arange(cols, dtype=F32)[:, None] * omega
    r_emb = jnp.concatenate([jnp.sin(r), jnp.cos(r)], axis=-1)
    c_emb = jnp.concatenate([jnp.sin(cc), jnp.cos(cc)], axis=-1)
    emb = jnp.concatenate([jnp.broadcast_to(r_emb[:, None, :], (rows, cols, dim // 2)),
                           jnp.broadcast_to(c_emb[None, :, :], (rows, cols, dim // 2))], axis=-1)
    return emb.reshape(rows * cols, dim)


def _pad_lanes(a):
    return jnp.pad(a, ((0, 0), (0, LANES - a.shape[1])))


def kernel(x_prompt, x_sample, c, state_delta, c_ctx, norm1_w, norm2_w, w_mod, b_mod, w_in, w_out, qkv_conv_w, delta_a_log, delta_dt_bias, delta_norm_w, sgu_norm_w, sgu_w, sgu_b, sconv_w, pool_w, pool_scale, ffn_w_gate, ffn_w_up, ffn_w_down, router_w, moe_w_gate, moe_w_up, moe_w_down, final_norm_w):
    bc, sc, d = x_prompt.shape
    bd, sd, _ = x_sample.shape
    depth = w_in.shape[0]
    mc, ml = bc * sc, bd * sd
    m = mc + ml
    w_a = H_A * LANES
    w_b, w_c, w_d = sgu_norm_w.shape[1], sconv_w.shape[1], pool_scale.shape[1]
    assert d == ROW_CHUNKS * LANES and mc % sd == 0 and w_a == w_b == w_c == w_d == d // 4
    assert delta_a_log.shape[1:] == (2, H_A) and delta_norm_w.shape[1] == LANES

    tm_mm = 512 if m % 512 == 0 else 256
    grid_w = 64
    pos = _sincos_2d(sd // grid_w, grid_w, d)
    x = jnp.concatenate([x_prompt.reshape(mc, d), (x_sample + pos[None]).reshape(ml, d)], axis=0)

    rm = -(-(1 + bd) // 8) * 8
    cond = jnp.zeros((rm, d), F32).at[0].set(c_ctx).at[1:1 + bd].set(c)
    mod_all = _mod_call(cond, w_mod, b_mod).reshape(depth, rm, 6, d)

    o0 = 3 * w_a
    o1 = o0 + w_a
    o3 = o1 + 4 * H_A
    o4 = o3 + 2 * w_b
    o5 = o4 + 3 * w_c
    blk0 = mc // sd
    w_main_all = jnp.concatenate([w_in[:, :, :o1], w_in[:, :, o3:]], axis=2).astype(BF16)
    w_ab_all = jnp.pad(w_in[:, :, o1:o3], ((0, 0), (0, 0), (0, LANES - (o3 - o1)))).astype(BF16)
    w_out_b = w_out.astype(BF16)
    ffn_b = [w.astype(BF16) for w in (ffn_w_gate, ffn_w_up, ffn_w_down)]
    moe_b = [w.astype(BF16) for w in (moe_w_gate, moe_w_up, moe_w_down)]
    ctx_states = []
    for l in range(depth):
        mod = mod_all[l]
        proj, ab = _inproj_call(x, norm1_w[l][None], mod, w_main_all, w_ab_all, l, mc, sd, tm_mm, 1024)

        conv_t = qkv_conv_w[l].T
        alog = _pad_lanes(delta_a_log[l].reshape(1, 2 * H_A))
        dtb = _pad_lanes(delta_dt_bias[l].reshape(1, 2 * H_A))
        nwa = delta_norm_w[l][None]
        a_ctx, s_ctx = _delta_call(proj, ab, conv_t, alog, dtb, nwa, None, l,
                                   t_len=sc, nseq=bc, blk0=0, write_state=True)
        (a_lat,) = _delta_call(proj, ab, conv_t, alog, dtb, nwa, state_delta, l,
                               t_len=sd, nseq=bd, blk0=blk0, write_state=False)
        ctx_states.append(s_ctx)

        def both_paths(kern, name, col_block, width_in, params, pspecs, width_out):
            return (_seq_call(kern, name, proj, col_block, width_in, params, pspecs,
                              t_len=sc, nseq=bc, blk0=0, width_out=width_out),
                    _seq_call(kern, name, proj, col_block, width_in, params, pspecs,
                              t_len=sd, nseq=bd, blk0=blk0, width_out=width_out))

        n_g = sgu_w.shape[1]
        b_ctx, b_lat = both_paths(
            _sgu_kernel, "sgu", o1 // (2 * w_b), 2 * w_b,
            [sgu_norm_w[l][None], sgu_w[l].astype(BF16), sgu_b[l].reshape(n_g, SGU_CHUNK, 1)],
            [pl.BlockSpec((1, w_b), lambda b: (0, 0)),
             pl.BlockSpec((n_g, SGU_CHUNK, SGU_CHUNK), lambda b: (0, 0, 0)),
             pl.BlockSpec((n_g, SGU_CHUNK, 1), lambda b: (0, 0, 0))], w_b)
        c_ctx_mix, c_lat = both_paths(
            _sconv_kernel, "sconv", (o1 + 2 * w_b) // (3 * w_c), 3 * w_c,
            [sconv_w[l].T], [pl.BlockSpec((3, w_c), lambda b: (0, 0))], w_c)
        n_gd = pool_w.shape[1]
        d_ctx, d_lat = both_paths(
            _pool_kernel, "pool", (o1 + 2 * w_b + 3 * w_c) // w_d, w_d,
            [pool_w[l].astype(BF16), pool_scale[l][None]],
            [pl.BlockSpec((n_gd, LANES, LANES), lambda b: (0, 0, 0)),
             pl.BlockSpec((1, w_d), lambda b: (0, 0))], w_d)

        x = _outproj_call(x, (a_ctx, b_ctx, c_ctx_mix, d_ctx), (a_lat, b_lat, c_lat, d_lat),
                          w_out_b, l, mod, mc, sd, tm_mm)

        jl = l // 2
        if l % 2 == 0:
            x = _ffn_call(x, norm2_w[l][None], mod, *ffn_b, jl, mc, sd, tm_mm, 512)
        else:
            x = _moe_layer(x, norm2_w[l][None], mod, _pad_lanes(router_w[jl]), *moe_b, jl,
                           mc, sd, tm_route=256, tm=512, tf=512, tc=256, gather_rows=512)

    y = _final_call(x, final_norm_w[None], tm_mm)
    y_prompt = y[:mc].reshape(bc, sc, d)
    y_sample = y[mc:].reshape(bd, sd, d)
    new_state = jnp.stack(ctx_states, axis=1).astype(x_prompt.dtype)
    return (y_prompt, y_sample, new_state)
```

```python
import functools

import jax
import jax.numpy as jnp
from jax import lax
from jax.experimental import pallas as pl
from jax.experimental.pallas import tpu as pltpu

F32 = jnp.float32
BF16 = jnp.bfloat16

LANES = 128
ROW_CHUNKS = 16
DELTA_CHUNK = 64
SGU_CHUNK = 128
H_A = 4
POOL_WINDOWS = (2, 4, 8, 16)
TOP_K = 2
VMEM_LIMIT = 56 * 1024 * 1024


def _cparams(*sem):
    return pltpu.CompilerParams(dimension_semantics=sem, vmem_limit_bytes=VMEM_LIMIT)


def _silu(x):
    return x / (1.0 + jnp.exp(-x))


def _sigmoid(x):
    return 1.0 / (1.0 + jnp.exp(-x))


def _softplus(x):
    return jnp.maximum(x, 0.0) + jnp.log1p(jnp.exp(-jnp.abs(x)))


def _gelu_tanh(x):
    return 0.5 * x * (1.0 + jnp.tanh(0.7978845608028654 * (x + 0.044715 * (x * x * x))))


def _rms(x, w, eps=1e-6):
    return x * lax.rsqrt(jnp.mean(x * x, axis=-1, keepdims=True) + eps) * w


def _dot(a, b):
    return jnp.dot(a, b, preferred_element_type=F32)


def _split(a):
    hi = a.astype(BF16)
    lo = (a - hi.astype(F32)).astype(BF16)
    return hi, lo


def _dot3(a, b):
    a_hi, a_lo = _split(a)
    b_hi, b_lo = _split(b)
    return _dot(a_hi, b_hi) + _dot(a_hi, b_lo) + _dot(a_lo, b_hi)


def _bmm(a, b):
    return jnp.einsum('nij,njk->nik', a, b, preferred_element_type=F32)


def _bmm16(a, b):
    return _bmm(a.astype(BF16), b.astype(BF16))


def _mod_kernel(c_ref, w_ref, b_ref, o_ref):
    a = _silu(c_ref[...]).astype(BF16)
    o_ref[...] = _dot(a, w_ref[...].astype(BF16)) + b_ref[...]


def _mod_call(cond, w_mod, b_mod, tn=1024):
    depth, d, n = w_mod.shape
    rm = cond.shape[0]
    return pl.pallas_call(
        _mod_kernel,
        grid=(depth, n // tn),
        in_specs=[pl.BlockSpec((rm, d), lambda l, j: (0, 0)),
                  pl.BlockSpec((None, d, tn), lambda l, j: (l, 0, j)),
                  pl.BlockSpec((None, 1, tn), lambda l, j: (l, 0, j))],
        out_specs=pl.BlockSpec((None, rm, tn), lambda l, j: (l, 0, j)),
        out_shape=jax.ShapeDtypeStruct((depth, rm, n), F32),
        compiler_params=_cparams("parallel", "parallel"),
        name="adaln_mod",
    )(cond, w_mod, b_mod.reshape(depth, 1, n))


def _mod_row_index(i, tm, mc, sd):
    r0 = i * tm
    return jnp.where(r0 < mc, 0, 1 + (r0 - mc) // sd)


def _inproj_kernel(x_ref, nw_ref, mod_ref, w_ref, wab_ref, proj_ref, ab_ref, h_ref):
    @pl.when(pl.program_id(1) == 0)
    def _():
        h = _rms(x_ref[...], nw_ref[...]) * (1.0 + mod_ref[1:2, :]) + mod_ref[0:1, :]
        hb = h.astype(BF16)
        h_ref[...] = hb
        ab_ref[...] = _dot(hb, wab_ref[...])

    proj_ref[...] = _dot(h_ref[...], w_ref[...])


def _inproj_call(x, nw, mod, w_main, w_ab, layer, mc, sd, tm, tn):
    m, d = x.shape
    n = w_main.shape[2]
    midx = functools.partial(_mod_row_index, tm=tm, mc=mc, sd=sd)
    return pl.pallas_call(
        _inproj_kernel,
        grid=(m // tm, n // tn),
        in_specs=[pl.BlockSpec((tm, d), lambda i, j: (i, 0)),
                  pl.BlockSpec((1, d), lambda i, j: (0, 0)),
                  pl.BlockSpec((None, 6, d), lambda i, j: (midx(i), 0, 0)),
                  pl.BlockSpec((None, d, tn), lambda i, j: (layer, 0, j)),
                  pl.BlockSpec((None, d, LANES), lambda i, j: (layer, 0, 0))],
        out_specs=[pl.BlockSpec((tm, tn), lambda i, j: (i, j)),
                   pl.BlockSpec((tm, LANES), lambda i, j: (i, 0))],
        out_shape=[jax.ShapeDtypeStruct((m, n), F32), jax.ShapeDtypeStruct((m, LANES), F32)],
        scratch_shapes=[pltpu.VMEM((tm, d), BF16)],
        compiler_params=_cparams("parallel", "arbitrary"),
        name="in_proj",
    )(x, nw, mod, w_main, w_ab)


def _shift_rows(x, d, row, t_len):
    if d == 0:
        return x
    y = pltpu.roll(x, (-d) % t_len, axis=0)
    ok = (row + d >= 0) & (row + d < t_len)
    return jnp.where(ok, y, 0.0)


def _conv3(x, w, row, t_len):
    return (_shift_rows(x, -1, row, t_len) * w[0:1, :] + x * w[1:2, :]
            + _shift_rows(x, 1, row, t_len) * w[2:3, :])


def _chunk_cumsum(g, row, t_len, reverse):
    pos = row % DELTA_CHUNK
    s = 1
    while s < DELTA_CHUNK:
        if reverse:
            g = g + jnp.where(pos < DELTA_CHUNK - s, pltpu.roll(g, t_len - s, axis=0), 0.0)
        else:
            g = g + jnp.where(pos >= s, pltpu.roll(g, s, axis=0), 0.0)
        s *= 2
    return g


def _unit_tri_inverse(lmat, r, c):
    eye = jnp.where(r == c, 1.0, 0.0)

    def same_block(b):
        return (r // b) == (c // b)

    x = jnp.where(same_block(8), -lmat, 0.0)
    x2 = _bmm16(x, x)
    x4 = _bmm16(x2, x2)
    p = _bmm16(_bmm16(eye + x, eye + x2), eye + x4)
    b = 8
    while b < DELTA_CHUNK:
        off = jnp.where(same_block(2 * b) & jnp.logical_not(same_block(b)), lmat, 0.0)
        pb = p.astype(BF16)
        p = p - _bmm16(_bmm16(pb, off), pb)
        b *= 2
    return p


def _delta_kernel(*refs, t_len, hps, has_s0, write_state):
    (q_ref, k_ref, v_ref, ga_ref, ab_ref, cq_ref, ck_ref, cv_ref, alog_ref, dtb_ref, nw_ref), rest = refs[:11], refs[11:]
    if has_s0:
        s0_ref, rest = rest[0], rest[1:]
    o_ref, rest = rest[0], rest[1:]
    if write_state:
        sout_ref = rest[0]

    n_ch = t_len // DELTA_CHUNK
    cl = DELTA_CHUNK
    row = lax.broadcasted_iota(jnp.int32, (t_len, LANES), 0)
    lane = lax.broadcasted_iota(jnp.int32, (t_len, LANES), 1)
    r64 = lax.broadcasted_iota(jnp.int32, (cl, cl), 0)
    c64 = lax.broadcasted_iota(jnp.int32, (cl, cl), 1)

    def l2n(x):
        return x * lax.rsqrt(jnp.sum(x * x, axis=-1, keepdims=True) + 1e-6)

    def column(a, idx):
        col = jnp.sum(jnp.where(lane == idx, a, 0.0), axis=1, keepdims=True)
        return jnp.broadcast_to(col, (t_len, LANES))

    def chunks(a):
        return a.reshape(n_ch, cl, LANES)

    ab = ab_ref[...]
    g_all = -jnp.exp(alog_ref[...]) * _softplus(ab + dtb_ref[...])
    b_all = _sigmoid(ab)

    chains = []
    for hh in range(hps):
        head = pl.program_id(1) * hps + hh
        cs = slice(hh * LANES, (hh + 1) * LANES)
        q = l2n(_silu(_conv3(q_ref[:, cs], cq_ref[:, cs], row, t_len))) * (LANES ** -0.5)
        k = l2n(_silu(_conv3(k_ref[:, cs], ck_ref[:, cs], row, t_len)))
        v = _silu(_conv3(v_ref[:, cs], cv_ref[:, cs], row, t_len))
        k3 = chunks(k)
        kb16 = k3.astype(BF16)
        kk = jnp.einsum('ncd,nsd->ncs', kb16, kb16, preferred_element_type=F32)
        qk = jnp.einsum('ncd,nsd->ncs', chunks(q).astype(BF16), kb16, preferred_element_type=F32)
        for direction in range(2):
            rev = direction == 1
            g = column(g_all, direction * H_A + head)
            beta = column(b_all, 2 * H_A + direction * H_A + head)
            gc = _chunk_cumsum(g, row, t_len, rev)
            gc3 = chunks(gc)
            gc_rows = jnp.swapaxes(gc3, 1, 2)[:, :cl, :]
            dmat = gc3[:, :, :cl] - gc_rows
            if rev:
                incl, strict = c64 >= r64, c64 > r64
            else:
                incl, strict = c64 <= r64, c64 < r64
            decay = jnp.exp(jnp.where(incl, dmat, 0.0))
            lmat = jnp.where(strict, chunks(beta)[:, :, :cl] * kk * decay, 0.0)
            tinv = _unit_tri_inverse(lmat, r64, c64).astype(BF16)
            egc = jnp.exp(gc)
            last = 0 if rev else cl - 1
            glast = gc3[:, last:last + 1, :]
            if has_s0:
                s_init = s0_ref[direction, hh]
            else:
                s_init = jnp.zeros((LANES, LANES), F32)
            chains.append(dict(
                hh=hh, direction=direction, s=s_init, outs=[None] * n_ch,
                order=list(range(n_ch - 1, -1, -1) if rev else range(n_ch)),
                attn=jnp.where(incl, qk * decay, 0.0).astype(BF16),
                u=_bmm16(tinv, chunks(v * beta)),
                w=_bmm16(tinv, chunks(k * beta * egc)).astype(BF16),
                qg=chunks(q * egc).astype(BF16),
                kdt=jnp.swapaxes(k3 * jnp.exp(glast - gc3), 1, 2).astype(BF16),
                gl=jnp.exp(glast)))

    for t in range(n_ch):
        for ch in chains:
            n = ch['order'][t]
            sb = ch['s'].astype(BF16)
            v_new = ch['u'][n] - _dot(ch['w'][n], sb)
            vb = v_new.astype(BF16)
            ch['outs'][n] = _dot(ch['qg'][n], sb) + _dot(ch['attn'][n], vb)
            ch['s'] = ch['s'] * ch['gl'][n] + _dot(ch['kdt'][n], vb)

    for ch in chains:
        if write_state:
            sout_ref[ch['direction'], ch['hh']] = ch['s']
    for hh in range(hps):
        cs = slice(hh * LANES, (hh + 1) * LANES)
        fwd, bwd = chains[2 * hh], chains[2 * hh + 1]
        o = jnp.concatenate([a + b for a, b in zip(fwd['outs'], bwd['outs'])], axis=0)
        o_ref[:, cs] = (_rms(o, nw_ref[...]) * _silu(ga_ref[:, cs])).astype(o_ref.dtype)


def _delta_call(proj, ab, conv_t, alog, dtb, nw, s0, layer, *, t_len, nseq, blk0, write_state, hps=2):
    has_s0 = s0 is not None
    w_a = H_A * LANES
    wh = hps * LANES
    nhb = H_A // hps

    def pspec(part):
        return pl.BlockSpec((t_len, wh), lambda b, h: (blk0 + b, part * nhb + h))

    def cspec(part):
        return pl.BlockSpec((3, wh), lambda b, h: (0, part * nhb + h))

    row1 = pl.BlockSpec((1, LANES), lambda b, h: (0, 0))
    in_specs = [pspec(0), pspec(1), pspec(2), pspec(3),
                pl.BlockSpec((t_len, LANES), lambda b, h: (blk0 + b, 0)),
                cspec(0), cspec(1), cspec(2), row1, row1, row1]
    args = [proj, proj, proj, proj, ab, conv_t, conv_t, conv_t, alog, dtb, nw]
    if has_s0:
        in_specs.append(pl.BlockSpec((None, None, 2, hps, LANES, LANES), lambda b, h: (b, layer, 0, h, 0, 0)))
        args.append(s0)
    out_specs = [pl.BlockSpec((t_len, wh), lambda b, h: (b, h))]
    out_shape = [jax.ShapeDtypeStruct((nseq * t_len, w_a), BF16)]
    if write_state:
        out_specs.append(pl.BlockSpec((None, 2, hps, LANES, LANES), lambda b, h: (b, 0, h, 0, 0)))
        out_shape.append(jax.ShapeDtypeStruct((nseq, 2, H_A, LANES, LANES), F32))

    return pl.pallas_call(
        functools.partial(_delta_kernel, t_len=t_len, hps=hps, has_s0=has_s0, write_state=write_state),
        grid=(nseq, nhb),
        in_specs=in_specs,
        out_specs=out_specs,
        out_shape=out_shape,
        compiler_params=_cparams("parallel", "parallel"),
        name=f"delta_T{t_len}",
    )(*args)


def _sgu_kernel(z_ref, nw_ref, w_ref, b_ref, o_ref, *, t_len):
    wb = o_ref.shape[1]
    z = _gelu_tanh(z_ref[...])
    u, v = z[:, :wb], z[:, wb:]
    mu = jnp.mean(v, axis=-1, keepdims=True)
    vc = v - mu
    vn = (vc * lax.rsqrt(jnp.mean(vc * vc, axis=-1, keepdims=True) + 1e-5) * nw_ref[...]).astype(BF16)
    for n in range(t_len // SGU_CHUNK):
        rs = slice(n * SGU_CHUNK, (n + 1) * SGU_CHUNK)
        for g in range(wb // LANES):
            cs = slice(g * LANES, (g + 1) * LANES)
            sp = _dot(w_ref[g], vn[rs, cs]) + b_ref[g]
            o_ref[rs, cs] = (u[rs, cs] * sp).astype(o_ref.dtype)


def _sconv_kernel(c_ref, w_ref, o_ref, *, t_len):
    wc = o_ref.shape[1]
    row = lax.broadcasted_iota(jnp.int32, (t_len, wc), 0)
    x = c_ref[...]
    y = _conv3(x[:, wc:2 * wc] * x[:, 2 * wc:], w_ref[...], row, t_len)
    o_ref[...] = (x[:, :wc] * y).astype(o_ref.dtype)


def _pool_kernel(p_ref, w_ref, sc_ref, o_ref, *, t_len):
    row = lax.broadcasted_iota(jnp.int32, (t_len, LANES), 0)
    for j, win in enumerate(POOL_WINDOWS):
        cs = slice(j * LANES, (j + 1) * LANES)
        x = p_ref[:, cs]
        half = win // 2
        acc = x
        for d in range(-half, win - half):
            if d != 0:
                acc = acc + _shift_rows(x, d, row, t_len)
        cnt = (jnp.minimum(row + (win - half), t_len) - jnp.maximum(row - half, 0)).astype(F32)
        pooled = acc / cnt - x
        o_ref[:, cs] = (_dot(pooled.astype(BF16), w_ref[j]) * sc_ref[:, cs]).astype(o_ref.dtype)


def _seq_call(kern, name, proj, col_block, width_in, params, pspecs, *, t_len, nseq, blk0, width_out):
    return pl.pallas_call(
        functools.partial(kern, t_len=t_len),
        grid=(nseq,),
        in_specs=[pl.BlockSpec((t_len, width_in), lambda b: (blk0 + b, col_block))] + pspecs,
        out_specs=pl.BlockSpec((t_len, width_out), lambda b: (b, 0)),
        out_shape=jax.ShapeDtypeStruct((nseq * t_len, width_out), BF16),
        compiler_params=_cparams("parallel"),
        name=f"{name}_T{t_len}",
    )(proj, *params)


def _outproj_kernel(x_ref, *refs, n_ctx_tiles):
    n_mix = (len(refs) - 3) // 2
    ctx_refs, lat_refs = refs[:n_mix], refs[n_mix:2 * n_mix]
    w_ref, mod_ref, o_ref = refs[2 * n_mix:]

    def run(mix_refs):
        acc = None
        for i, m_ref in enumerate(mix_refs):
            wq = m_ref.shape[1]
            part = _dot(m_ref[...], w_ref[i * wq:(i + 1) * wq, :])
            acc = part if acc is None else acc + part
        o_ref[...] = x_ref[...] + mod_ref[2:3, :] * acc

    is_ctx = pl.program_id(0) < n_ctx_tiles
    pl.when(is_ctx)(lambda: run(ctx_refs))
    pl.when(jnp.logical_not(is_ctx))(lambda: run(lat_refs))


def _outproj_call(x, mixes_ctx, mixes_lat, w_out, layer, mod, mc, sd, tm):
    m, d = x.shape
    wq = mixes_ctx[0].shape[1]
    nct = mc // tm
    midx = functools.partial(_mod_row_index, tm=tm, mc=mc, sd=sd)
    cspec = pl.BlockSpec((tm, wq), lambda i: (jnp.minimum(i, nct - 1), 0))
    lspec = pl.BlockSpec((tm, wq), lambda i: (jnp.maximum(i - nct, 0), 0))
    n_mix = len(mixes_ctx)
    return pl.pallas_call(
        functools.partial(_outproj_kernel, n_ctx_tiles=nct),
        grid=(m // tm,),
        in_specs=[pl.BlockSpec((tm, d), lambda i: (i, 0))] + [cspec] * n_mix + [lspec] * n_mix
                 + [pl.BlockSpec((None, d, d), lambda i: (layer, 0, 0)),
                    pl.BlockSpec((None, 6, d), lambda i: (midx(i), 0, 0))],
        out_specs=pl.BlockSpec((tm, d), lambda i: (i, 0)),
        out_shape=jax.ShapeDtypeStruct((m, d), F32),
        compiler_params=_cparams("parallel"),
        name="out_proj",
    )(x, *mixes_ctx, *mixes_lat, w_out, mod)


def _ffn_kernel(x_ref, nw_ref, mod_ref, wg_ref, wu_ref, wd_ref, o_ref, h_ref, acc_ref):
    j = pl.program_id(1)

    @pl.when(j == 0)
    def _():
        h = _rms(x_ref[...], nw_ref[...]) * (1.0 + mod_ref[4:5, :]) + mod_ref[3:4, :]
        h_ref[...] = h.astype(BF16)
        acc_ref[...] = jnp.zeros_like(acc_ref)

    h = h_ref[...]
    a = (_silu(_dot(h, wg_ref[...])) * _dot(h, wu_ref[...])).astype(BF16)
    acc_ref[...] += _dot(a, wd_ref[...])

    @pl.when(j == pl.num_programs(1) - 1)
    def _():
        o_ref[...] = x_ref[...] + mod_ref[5:6, :] * acc_ref[...]


def _ffn_call(x, nw, mod, wg, wu, wd, layer, mc, sd, tm, tf):
    m, d = x.shape
    f = wg.shape[2]
    midx = functools.partial(_mod_row_index, tm=tm, mc=mc, sd=sd)
    return pl.pallas_call(
        _ffn_kernel,
        grid=(m // tm, f // tf),
        in_specs=[pl.BlockSpec((tm, d), lambda i, j: (i, 0)),
                  pl.BlockSpec((1, d), lambda i, j: (0, 0)),
                  pl.BlockSpec((None, 6, d), lambda i, j: (midx(i), 0, 0)),
                  pl.BlockSpec((None, d, tf), lambda i, j: (layer, 0, j)),
                  pl.BlockSpec((None, d, tf), lambda i, j: (layer, 0, j)),
                  pl.BlockSpec((None, tf, d), lambda i, j: (layer, j, 0))],
        out_specs=pl.BlockSpec((tm, d), lambda i, j: (i, 0)),
        out_shape=jax.ShapeDtypeStruct((m, d), F32),
        scratch_shapes=[pltpu.VMEM((tm, d), BF16), pltpu.VMEM((tm, d), F32)],
        compiler_params=_cparams("parallel", "arbitrary"),
        name="ffn_dense",
    )(x, nw, mod, wg, wu, wd)


def _route_kernel(x_ref, nw_ref, mod_ref, rw_ref, h_ref, r_ref, *, n_experts):
    tm = x_ref.shape[0]
    h = _rms(x_ref[...], nw_ref[...]) * (1.0 + mod_ref[4:5, :]) + mod_ref[3:4, :]
    for cb in range(ROW_CHUNKS):
        h_ref[pl.ds(cb, tm, stride=ROW_CHUNKS), :] = h[:, cb * LANES:(cb + 1) * LANES]
    lane = lax.broadcasted_iota(jnp.int32, (tm, LANES), 1).astype(F32)
    neg = jnp.float32(-jnp.inf)
    logits = jnp.where(lane < n_experts, _dot3(h, rw_ref[...]), neg)
    m1 = jnp.max(logits, axis=-1, keepdims=True)
    i1 = jnp.min(jnp.where(logits == m1, lane, float(LANES)), axis=-1, keepdims=True)
    rest = jnp.where(lane == i1, neg, logits)
    m2 = jnp.max(rest, axis=-1, keepdims=True)
    i2 = jnp.min(jnp.where(rest == m2, lane, float(LANES)), axis=-1, keepdims=True)
    e2 = jnp.exp(m2 - m1)
    den = 1.0 + e2
    r_ref[...] = jnp.where(lane == 0.0, i1,
                           jnp.where(lane == 1.0, i2,
                                     jnp.where(lane == 2.0, 1.0 / den, jnp.where(lane == 3.0, e2 / den, 0.0))))


def _route_call(x, nw, mod, rw, mc, sd, tm, n_experts):
    m, d = x.shape
    midx = functools.partial(_mod_row_index, tm=tm, mc=mc, sd=sd)
    return pl.pallas_call(
        functools.partial(_route_kernel, n_experts=n_experts),
        grid=(m // tm,),
        in_specs=[pl.BlockSpec((tm, d), lambda i: (i, 0)),
                  pl.BlockSpec((1, d), lambda i: (0, 0)),
                  pl.BlockSpec((None, 6, d), lambda i: (midx(i), 0, 0)),
                  pl.BlockSpec((d, LANES), lambda i: (0, 0))],
        out_specs=[pl.BlockSpec((tm * ROW_CHUNKS, LANES), lambda i: (i, 0)),
                   pl.BlockSpec((tm, LANES), lambda i: (i, 0))],
        out_shape=[jax.ShapeDtypeStruct((m * ROW_CHUNKS, LANES), F32),
                   jax.ShapeDtypeStruct((m, LANES), F32)],
        compiler_params=_cparams("parallel"),
        name="moe_route",
    )(x, nw, mod, rw)


def _row_copy(src_hbm, dst_ref, src_row, dst_row, sem):
    s0 = pl.multiple_of(src_row * ROW_CHUNKS, ROW_CHUNKS)
    d0 = pl.multiple_of(dst_row * ROW_CHUNKS, ROW_CHUNKS)
    return pltpu.make_async_copy(src_hbm.at[pl.ds(s0, ROW_CHUNKS), :], dst_ref.at[pl.ds(d0, ROW_CHUNKS), :], sem)


def _gather_kernel(nrows_ref, idx_ref, src_hbm, o_ref, sem, *, rows):
    used = pl.program_id(0) * rows < nrows_ref[0]

    @pl.when(used)
    def _():
        def start(r, carry):
            _row_copy(src_hbm, o_ref, idx_ref[0, r], r, sem).start()
            return carry

        def wait(r, carry):
            _row_copy(src_hbm, o_ref, idx_ref[0, r], r, sem).wait()
            return carry

        lax.fori_loop(0, rows, start, 0)
        lax.fori_loop(0, rows, wait, 0)

    @pl.when(jnp.logical_not(used))
    def _():
        o_ref[...] = jnp.zeros_like(o_ref)


def _gather_call(h_rows, src_idx, nrows, rows):
    p = src_idx.shape[0]
    return pl.pallas_call(
        functools.partial(_gather_kernel, rows=rows),
        grid_spec=pltpu.PrefetchScalarGridSpec(
            num_scalar_prefetch=1,
            grid=(p // rows,),
            in_specs=[pl.BlockSpec((None, 1, rows), lambda i, n: (i, 0, 0), memory_space=pltpu.SMEM),
                      pl.BlockSpec(memory_space=pl.ANY)],
            out_specs=pl.BlockSpec((rows * ROW_CHUNKS, LANES), lambda i, n: (i, 0)),
            scratch_shapes=[pltpu.SemaphoreType.DMA(())]),
        out_shape=jax.ShapeDtypeStruct((p * ROW_CHUNKS, LANES), F32),
        compiler_params=_cparams("arbitrary"),
        name="moe_gather",
    )(nrows, src_idx.reshape(p // rows, 1, rows), h_rows)


def _experts_kernel(te_ref, nt_ref, xs_ref, wg_ref, wu_ref, wd_ref, o_ref, x_ref, acc_ref):
    i, j = pl.program_id(0), pl.program_id(1)
    tm = x_ref.shape[0]

    @pl.when(i < nt_ref[0])
    def _():
        @pl.when(j == 0)
        def _():
            for cb in range(ROW_CHUNKS):
                x_ref[:, cb * LANES:(cb + 1) * LANES] = xs_ref[pl.ds(cb, tm, stride=ROW_CHUNKS), :].astype(BF16)
            acc_ref[...] = jnp.zeros_like(acc_ref)

        x = x_ref[...]
        a = (_silu(_dot(x, wg_ref[...].astype(BF16))) * _dot(x, wu_ref[...].astype(BF16))).astype(BF16)
        acc_ref[...] += _dot(a, wd_ref[...].astype(BF16))

        @pl.when(j == pl.num_programs(1) - 1)
        def _():
            for cb in range(ROW_CHUNKS):
                o_ref[pl.ds(cb, tm, stride=ROW_CHUNKS), :] = acc_ref[:, cb * LANES:(cb + 1) * LANES]

    @pl.when((i >= nt_ref[0]) & (j == pl.num_programs(1) - 1))
    def _():
        o_ref[...] = jnp.zeros_like(o_ref)


def _experts_call(tile_e, ntiles, xs, wg, wu, wd, layer, tm, tf):
    d, f = wg.shape[2:]
    p = xs.shape[0] // ROW_CHUNKS
    nf = f // tf

    def row_blk(i, j, te, nt):
        return (jnp.minimum(i, nt[0] - 1), 0)

    def jj(i, j, nt):
        return jnp.where(i < nt[0], j, nf - 1)

    def w_in_blk(i, j, te, nt):
        return (layer, te[i], 0, jj(i, j, nt))

    def w_out_blk(i, j, te, nt):
        return (layer, te[i], jj(i, j, nt), 0)

    return pl.pallas_call(
        _experts_kernel,
        grid_spec=pltpu.PrefetchScalarGridSpec(
            num_scalar_prefetch=2,
            grid=(p // tm, nf),
            in_specs=[pl.BlockSpec((tm * ROW_CHUNKS, LANES), row_blk),
                      pl.BlockSpec((None, None, d, tf), w_in_blk),
                      pl.BlockSpec((None, None, d, tf), w_in_blk),
                      pl.BlockSpec((None, None, tf, d), w_out_blk)],
            out_specs=pl.BlockSpec((tm * ROW_CHUNKS, LANES), lambda i, j, te, nt: (i, 0)),
            scratch_shapes=[pltpu.VMEM((tm, d), BF16), pltpu.VMEM((tm, d), F32)]),
        out_shape=jax.ShapeDtypeStruct((p * ROW_CHUNKS, LANES), F32),
        compiler_params=_cparams("arbitrary", "arbitrary"),
        name="moe_experts",
    )(tile_e, ntiles, xs, wg, wu, wd)


def _combine_kernel(idx_ref, x_ref, r_ref, mod_ref, ys_hbm, o_ref, buf_ref, sem):
    tc = x_ref.shape[0]
    n_rows = TOP_K * tc

    def start(r, carry):
        _row_copy(ys_hbm, buf_ref, idx_ref[0, r], r, sem).start()
        return carry

    def wait(r, carry):
        _row_copy(ys_hbm, buf_ref, idx_ref[0, r], r, sem).wait()
        return carry

    lax.fori_loop(0, n_rows, start, 0)
    lax.fori_loop(0, n_rows, wait, 0)
    w1 = r_ref[:, 2:3]
    w2 = r_ref[:, 3:4]
    for cb in range(ROW_CHUNKS):
        cs = slice(cb * LANES, (cb + 1) * LANES)
        y1 = buf_ref[pl.ds(cb, tc, stride=ROW_CHUNKS), :]
        y2 = buf_ref[pl.ds(tc * ROW_CHUNKS + cb, tc, stride=ROW_CHUNKS), :]
        o_ref[:, cs] = x_ref[:, cs] + mod_ref[5:6, cs] * (y1 * w1 + y2 * w2)


def _combine_call(dest, x, route, mod, ys, mc, sd, tc):
    m, d = x.shape
    midx = functools.partial(_mod_row_index, tm=tc, mc=mc, sd=sd)
    return pl.pallas_call(
        _combine_kernel,
        grid=(m // tc,),
        in_specs=[pl.BlockSpec((None, 1, TOP_K * tc), lambda i: (i, 0, 0), memory_space=pltpu.SMEM),
                  pl.BlockSpec((tc, d), lambda i: (i, 0)),
                  pl.BlockSpec((tc, LANES), lambda i: (i, 0)),
                  pl.BlockSpec((None, 6, d), lambda i: (midx(i), 0, 0)),
                  pl.BlockSpec(memory_space=pl.ANY)],
        out_specs=pl.BlockSpec((tc, d), lambda i: (i, 0)),
        out_shape=jax.ShapeDtypeStruct((m, d), F32),
        scratch_shapes=[pltpu.VMEM((TOP_K * tc * ROW_CHUNKS, LANES), F32), pltpu.SemaphoreType.DMA(())],
        compiler_params=_cparams("arbitrary"),
        name="moe_combine",
    )(dest, x, route, mod, ys)


def _moe_plan(route, n_experts, tm, n_tiles, tc):
    m = route.shape[0]
    e_flat = jnp.concatenate([route[:, 0], route[:, 1]]).astype(jnp.int32)
    onehot = (e_flat[:, None] == jnp.arange(n_experts, dtype=jnp.int32)[None, :]).astype(jnp.int32)
    csum = jnp.cumsum(onehot, axis=0)
    cnt = csum[-1]
    rank = jnp.take_along_axis(csum, e_flat[:, None], axis=1)[:, 0] - 1
    gsz = ((cnt + tm - 1) // tm) * tm
    off_end = jnp.cumsum(gsz)
    dest = (off_end - gsz)[e_flat] + rank
    n_rows = off_end[-1]
    n_used = n_rows // tm
    tiles = jnp.arange(n_tiles, dtype=jnp.int32)
    tile_e = jnp.sum((off_end[None, :] <= (tiles * tm)[:, None]).astype(jnp.int32), axis=1)
    tile_e = jnp.minimum(tile_e, n_experts - 1)
    tile_e = jnp.where(tiles < n_used, tile_e, tile_e[jnp.maximum(n_used - 1, 0)])
    tok = jnp.tile(jnp.arange(m, dtype=jnp.int32), TOP_K)
    src = jnp.zeros((n_tiles * tm,), jnp.int32).at[dest].set(tok)
    dest_tiles = jnp.concatenate([dest[:m].reshape(m // tc, 1, tc), dest[m:].reshape(m // tc, 1, tc)], axis=2)
    return src, dest_tiles, tile_e, n_used.reshape(1).astype(jnp.int32), n_rows.reshape(1).astype(jnp.int32)


def _moe_layer(x, nw, mod, rw, wg, wu, wd, layer, mc, sd, tm_route, tm, tf, tc, gather_rows):
    m, d = x.shape
    n_experts = wg.shape[1]
    p_rows = TOP_K * m + n_experts * tm
    p_rows = -(-p_rows // gather_rows) * gather_rows
    n_tiles = p_rows // tm
    h_rows, route = _route_call(x, nw, mod, rw, mc, sd, tm_route, n_experts)
    src, dest_tiles, tile_e, n_used, n_rows = _moe_plan(route, n_experts, tm, n_tiles, tc)
    xs = _gather_call(h_rows, src, n_rows, gather_rows)
    ys = _experts_call(tile_e, n_used, xs, wg, wu, wd, layer, tm, tf)
    return _combine_call(dest_tiles, x, route, mod, ys, mc, sd, tc)


def _final_kernel(x_ref, w_ref, o_ref):
    o_ref[...] = _rms(x_ref[...], w_ref[...])


def _final_call(x, w, tm):
    m, d = x.shape
    return pl.pallas_call(
        _final_kernel,
        grid=(m // tm,),
        in_specs=[pl.BlockSpec((tm, d), lambda i: (i, 0)), pl.BlockSpec((1, d), lambda i: (0, 0))],
        out_specs=pl.BlockSpec((tm, d), lambda i: (i, 0)),
        out_shape=jax.ShapeDtypeStruct((m, d), F32),
        compiler_params=_cparams("parallel"),
        name="final_norm",
    )(x, w)


def _sincos_2d(rows, cols, dim):
    quarter = dim // 4
    omega = 1.0 / (10000.0 ** (jnp.arange(quarter, dtype=F32) / quarter))
    r = jnp.arange(rows, dtype=F32)[:, None] * omega
    cc = jnp.arange(cols, dtype=F32)[:, None] * omega
    r_emb = jnp.concatenate([jnp.sin(r), jnp.cos(r)], axis=-1)
    c_emb = jnp.concatenate([jnp.sin(cc), jnp.cos(cc)], axis=-1)
    emb = jnp.concatenate([jnp.broadcast_to(r_emb[:, None, :], (rows, cols, dim // 2)),
                           jnp.broadcast_to(c_emb[None, :, :], (rows, cols, dim // 2))], axis=-1)
    return emb.reshape(rows * cols, dim)


def _pad_lanes(a):
    return jnp.pad(a, ((0, 0), (0, LANES - a.shape[1])))


def kernel(x_prompt, x_sample, c, state_delta, c_ctx, norm1_w, norm2_w, w_mod, b_mod, w_in, w_out, qkv_conv_w, delta_a_log, delta_dt_bias, delta_norm_w, sgu_norm_w, sgu_w, sgu_b, sconv_w, pool_w, pool_scale, ffn_w_gate, ffn_w_up, ffn_w_down, router_w, moe_w_gate, moe_w_up, moe_w_down, final_norm_w):
    bc, sc, d = x_prompt.shape
    bd, sd, _ = x_sample.shape
    depth = w_in.shape[0]
    mc, ml = bc * sc, bd * sd
    m = mc + ml
    w_a = H_A * LANES
    w_b, w_c, w_d = sgu_norm_w.shape[1], sconv_w.shape[1], pool_scale.shape[1]
    assert d == ROW_CHUNKS * LANES and mc % sd == 0 and w_a == w_b == w_c == w_d == d // 4
    assert delta_a_log.shape[1:] == (2, H_A) and delta_norm_w.shape[1] == LANES

    tm_mm = 512 if m % 512 == 0 else 256
    grid_w = 64
    pos = _sincos_2d(sd // grid_w, grid_w, d)
    x = jnp.concatenate([x_prompt.reshape(mc, d), (x_sample + pos[None]).reshape(ml, d)], axis=0)

    rm = -(-(1 + bd) // 8) * 8
    cond = jnp.zeros((rm, d), F32).at[0].set(c_ctx).at[1:1 + bd].set(c)
    mod_all = _mod_call(cond, w_mod, b_mod).reshape(depth, rm, 6, d)

    o0 = 3 * w_a
    o1 = o0 + w_a
    o3 = o1 + 4 * H_A
    o4 = o3 + 2 * w_b
    o5 = o4 + 3 * w_c
    blk0 = mc // sd
    w_main_all = jnp.concatenate([w_in[:, :, :o1], w_in[:, :, o3:]], axis=2).astype(BF16)
    w_ab_all = jnp.pad(w_in[:, :, o1:o3], ((0, 0), (0, 0), (0, LANES - (o3 - o1)))).astype(BF16)
    w_out_b = w_out.astype(BF16)
    ffn_b = [w.astype(BF16) for w in (ffn_w_gate, ffn_w_up, ffn_w_down)]
    moe_w = (moe_w_gate, moe_w_up, moe_w_down)
    ctx_states = []
    for l in range(depth):
        mod = mod_all[l]
        tm_in = 1024 if (mc % 1024 == 0 and sd % 1024 == 0) else tm_mm
        proj, ab = _inproj_call(x, norm1_w[l][None], mod, w_main_all, w_ab_all, l, mc, sd, tm_in, 1024)

        conv_t = qkv_conv_w[l].T
        alog = _pad_lanes(delta_a_log[l].reshape(1, 2 * H_A))
        dtb = _pad_lanes(delta_dt_bias[l].reshape(1, 2 * H_A))
        nwa = delta_norm_w[l][None]
        a_ctx, s_ctx = _delta_call(proj, ab, conv_t, alog, dtb, nwa, None, l,
                                   t_len=sc, nseq=bc, blk0=0, write_state=True)
        (a_lat,) = _delta_call(proj, ab, conv_t, alog, dtb, nwa, state_delta, l,
                               t_len=sd, nseq=bd, blk0=blk0, write_state=False)
        ctx_states.append(s_ctx)

        def both_paths(kern, name, col_block, width_in, params, pspecs, width_out):
            return (_seq_call(kern, name, proj, col_block, width_in, params, pspecs,
                              t_len=sc, nseq=bc, blk0=0, width_out=width_out),
                    _seq_call(kern, name, proj, col_block, width_in, params, pspecs,
                              t_len=sd, nseq=bd, blk0=blk0, width_out=width_out))

        n_g = sgu_w.shape[1]
        b_ctx, b_lat = both_paths(
            _sgu_kernel, "sgu", o1 // (2 * w_b), 2 * w_b,
            [sgu_norm_w[l][None], sgu_w[l].astype(BF16), sgu_b[l].reshape(n_g, SGU_CHUNK, 1)],
            [pl.BlockSpec((1, w_b), lambda b: (0, 0)),
             pl.BlockSpec((n_g, SGU_CHUNK, SGU_CHUNK), lambda b: (0, 0, 0)),
             pl.BlockSpec((n_g, SGU_CHUNK, 1), lambda b: (0, 0, 0))], w_b)
        c_ctx_mix, c_lat = both_paths(
            _sconv_kernel, "sconv", (o1 + 2 * w_b) // (3 * w_c), 3 * w_c,
            [sconv_w[l].T], [pl.BlockSpec((3, w_c), lambda b: (0, 0))], w_c)
        n_gd = pool_w.shape[1]
        d_ctx, d_lat = both_paths(
            _pool_kernel, "pool", (o1 + 2 * w_b + 3 * w_c) // w_d, w_d,
            [pool_w[l].astype(BF16), pool_scale[l][None]],
            [pl.BlockSpec((n_gd, LANES, LANES), lambda b: (0, 0, 0)),
             pl.BlockSpec((1, w_d), lambda b: (0, 0))], w_d)

        x = _outproj_call(x, (a_ctx, b_ctx, c_ctx_mix, d_ctx), (a_lat, b_lat, c_lat, d_lat),
                          w_out_b, l, mod, mc, sd, tm_mm)

        jl = l // 2
        if l % 2 == 0:
            x = _ffn_call(x, norm2_w[l][None], mod, *ffn_b, jl, mc, sd, tm_mm, 512)
        else:
            x = _moe_layer(x, norm2_w[l][None], mod, _pad_lanes(router_w[jl]), *moe_w, jl,
                           mc, sd, tm_route=256, tm=512, tf=512, tc=256, gather_rows=512)

    y = _final_call(x, final_norm_w[None], tm_mm)
    y_prompt = y[:mc].reshape(bc, sc, d)
    y_sample = y[mc:].reshape(bd, sd, d)
    new_state = jnp.stack(ctx_states, axis=1).astype(x_prompt.dtype)
    return (y_prompt, y_sample, new_state)
```

```python
import functools

import jax
import jax.numpy as jnp
from jax import lax
from jax.experimental import pallas as pl
from jax.experimental.pallas import tpu as pltpu

F32 = jnp.float32
BF16 = jnp.bfloat16

LANES = 128
ROW_CHUNKS = 16
ROW_PITCH = 24
DELTA_CHUNK = 64
SGU_CHUNK = 128
H_A = 4
POOL_WINDOWS = (2, 4, 8, 16)
TOP_K = 2
VMEM_LIMIT = 56 * 1024 * 1024


def _cparams(*sem):
    return pltpu.CompilerParams(dimension_semantics=sem, vmem_limit_bytes=VMEM_LIMIT)


def _silu(x):
    return x / (1.0 + jnp.exp(-x))


def _sigmoid(x):
    return 1.0 / (1.0 + jnp.exp(-x))


def _softplus(x):
    return jnp.maximum(x, 0.0) + jnp.log1p(jnp.exp(-jnp.abs(x)))


def _gelu_tanh(x):
    return 0.5 * x * (1.0 + jnp.tanh(0.7978845608028654 * (x + 0.044715 * (x * x * x))))


def _rms(x, w, eps=1e-6):
    return x * lax.rsqrt(jnp.mean(x * x, axis=-1, keepdims=True) + eps) * w


def _dot(a, b):
    return jnp.dot(a, b, preferred_element_type=F32)


def _split(a):
    hi = a.astype(BF16)
    lo = (a - hi.astype(F32)).astype(BF16)
    return hi, lo


def _dot3(a, b):
    a_hi, a_lo = _split(a)
    b_hi, b_lo = _split(b)
    return _dot(a_hi, b_hi) + _dot(a_hi, b_lo) + _dot(a_lo, b_hi)


def _bmm(a, b):
    return jnp.einsum('nij,njk->nik', a, b, preferred_element_type=F32)


def _bmm16(a, b):
    return _bmm(a.astype(BF16), b.astype(BF16))


def _mod_kernel(c_ref, w_ref, b_ref, o_ref):
    a = _silu(c_ref[...]).astype(BF16)
    o_ref[...] = _dot(a, w_ref[...].astype(BF16)) + b_ref[...]


def _mod_call(cond, w_mod, b_mod, tn=1024):
    depth, d, n = w_mod.shape
    rm = cond.shape[0]
    return pl.pallas_call(
        _mod_kernel,
        grid=(depth, n // tn),
        in_specs=[pl.BlockSpec((rm, d), lambda l, j: (0, 0)),
                  pl.BlockSpec((None, d, tn), lambda l, j: (l, 0, j)),
                  pl.BlockSpec((None, 1, tn), lambda l, j: (l, 0, j))],
        out_specs=pl.BlockSpec((None, rm, tn), lambda l, j: (l, 0, j)),
        out_shape=jax.ShapeDtypeStruct((depth, rm, n), F32),
        compiler_params=_cparams("parallel", "parallel"),
        name="adaln_mod",
    )(cond, w_mod, b_mod.reshape(depth, 1, n))


def _mod_row_index(i, tm, mc, sd):
    r0 = i * tm
    return jnp.where(r0 < mc, 0, 1 + (r0 - mc) // sd)


def _inproj_kernel(x_ref, nw_ref, mod_ref, w_ref, wab_ref, proj_ref, ab_ref, h_ref):
    @pl.when(pl.program_id(1) == 0)
    def _():
        h = _rms(x_ref[...], nw_ref[...]) * (1.0 + mod_ref[1:2, :]) + mod_ref[0:1, :]
        hb = h.astype(BF16)
        h_ref[...] = hb
        ab_ref[...] = _dot(hb, wab_ref[...])

    proj_ref[...] = _dot(h_ref[...], w_ref[...])


def _inproj_call(x, nw, mod, w_main, w_ab, layer, mc, sd, tm, tn):
    m, d = x.shape
    n = w_main.shape[2]
    midx = functools.partial(_mod_row_index, tm=tm, mc=mc, sd=sd)
    return pl.pallas_call(
        _inproj_kernel,
        grid=(m // tm, n // tn),
        in_specs=[pl.BlockSpec((tm, d), lambda i, j: (i, 0)),
                  pl.BlockSpec((1, d), lambda i, j: (0, 0)),
                  pl.BlockSpec((None, 6, d), lambda i, j: (midx(i), 0, 0)),
                  pl.BlockSpec((None, d, tn), lambda i, j: (layer, 0, j)),
                  pl.BlockSpec((None, d, LANES), lambda i, j: (layer, 0, 0))],
        out_specs=[pl.BlockSpec((tm, tn), lambda i, j: (i, j)),
                   pl.BlockSpec((tm, LANES), lambda i, j: (i, 0))],
        out_shape=[jax.ShapeDtypeStruct((m, n), F32), jax.ShapeDtypeStruct((m, LANES), F32)],
        scratch_shapes=[pltpu.VMEM((tm, d), BF16)],
        compiler_params=_cparams("parallel", "arbitrary"),
        name="in_proj",
    )(x, nw, mod, w_main, w_ab)


def _shift_rows(x, d, row, t_len):
    if d == 0:
        return x
    y = pltpu.roll(x, (-d) % t_len, axis=0)
    ok = (row + d >= 0) & (row + d < t_len)
    return jnp.where(ok, y, 0.0)


def _conv3(x, w, row, t_len):
    return (_shift_rows(x, -1, row, t_len) * w[0:1, :] + x * w[1:2, :]
            + _shift_rows(x, 1, row, t_len) * w[2:3, :])


def _chunk_cumsum(g, row, t_len, reverse):
    pos = row % DELTA_CHUNK
    s = 1
    while s < DELTA_CHUNK:
        if reverse:
            g = g + jnp.where(pos < DELTA_CHUNK - s, pltpu.roll(g, t_len - s, axis=0), 0.0)
        else:
            g = g + jnp.where(pos >= s, pltpu.roll(g, s, axis=0), 0.0)
        s *= 2
    return g


def _unit_tri_inverse(lmat, r, c):
    eye = jnp.where(r == c, 1.0, 0.0)

    def same_block(b):
        return (r // b) == (c // b)

    x = jnp.where(same_block(8), -lmat, 0.0)
    x2 = _bmm16(x, x)
    x4 = _bmm16(x2, x2)
    p = _bmm16(_bmm16(eye + x, eye + x2), eye + x4)
    b = 8
    while b < DELTA_CHUNK:
        off = jnp.where(same_block(2 * b) & jnp.logical_not(same_block(b)), lmat, 0.0)
        pb = p.astype(BF16)
        p = p - _bmm16(_bmm16(pb, off), pb)
        b *= 2
    return p


def _delta_kernel(*refs, t_len, hps, has_s0, write_state):
    (q_ref, k_ref, v_ref, ga_ref, ab_ref, cq_ref, ck_ref, cv_ref, alog_ref, dtb_ref, nw_ref), rest = refs[:11], refs[11:]
    if has_s0:
        s0_ref, rest = rest[0], rest[1:]
    o_ref, rest = rest[0], rest[1:]
    if write_state:
        sout_ref = rest[0]

    n_ch = t_len // DELTA_CHUNK
    cl = DELTA_CHUNK
    row = lax.broadcasted_iota(jnp.int32, (t_len, LANES), 0)
    lane = lax.broadcasted_iota(jnp.int32, (t_len, LANES), 1)
    r64 = lax.broadcasted_iota(jnp.int32, (cl, cl), 0)
    c64 = lax.broadcasted_iota(jnp.int32, (cl, cl), 1)

    def l2n(x):
        return x * lax.rsqrt(jnp.sum(x * x, axis=-1, keepdims=True) + 1e-6)

    def column(a, idx):
        col = jnp.sum(jnp.where(lane == idx, a, 0.0), axis=1, keepdims=True)
        return jnp.broadcast_to(col, (t_len, LANES))

    def chunks(a):
        return a.reshape(n_ch, cl, LANES)

    ab = ab_ref[...]
    g_all = -jnp.exp(alog_ref[...]) * _softplus(ab + dtb_ref[...])
    b_all = _sigmoid(ab)

    chains = []
    for hh in range(hps):
        head = pl.program_id(1) * hps + hh
        cs = slice(hh * LANES, (hh + 1) * LANES)
        q = l2n(_silu(_conv3(q_ref[:, cs], cq_ref[:, cs], row, t_len))) * (LANES ** -0.5)
        k = l2n(_silu(_conv3(k_ref[:, cs], ck_ref[:, cs], row, t_len)))
        v = _silu(_conv3(v_ref[:, cs], cv_ref[:, cs], row, t_len))
        k3 = chunks(k)
        kb16 = k3.astype(BF16)
        kk = jnp.einsum('ncd,nsd->ncs', kb16, kb16, preferred_element_type=F32)
        qk = jnp.einsum('ncd,nsd->ncs', chunks(q).astype(BF16), kb16, preferred_element_type=F32)
        for direction in range(2):
            rev = direction == 1
            g = column(g_all, direction * H_A + head)
            beta = column(b_all, 2 * H_A + direction * H_A + head)
            gc = _chunk_cumsum(g, row, t_len, rev)
            gc3 = chunks(gc)
            gc_rows = jnp.swapaxes(gc3, 1, 2)[:, :cl, :]
            dmat = gc3[:, :, :cl] - gc_rows
            if rev:
                incl, strict = c64 >= r64, c64 > r64
            else:
                incl, strict = c64 <= r64, c64 < r64
            decay = jnp.exp(jnp.where(incl, dmat, 0.0))
            lmat = jnp.where(strict, chunks(beta)[:, :, :cl] * kk * decay, 0.0)
            tinv = _unit_tri_inverse(lmat, r64, c64).astype(BF16)
            egc = jnp.exp(gc)
            last = 0 if rev else cl - 1
            glast = gc3[:, last:last + 1, :]
            if has_s0:
                s_init = s0_ref[direction, hh]
            else:
                s_init = jnp.zeros((LANES, LANES), F32)
            chains.append(dict(
                hh=hh, direction=direction, s=s_init, outs=[None] * n_ch,
                order=list(range(n_ch - 1, -1, -1) if rev else range(n_ch)),
                attn=jnp.where(incl, qk * decay, 0.0).astype(BF16),
                u=_bmm16(tinv, chunks(v * beta)),
                w=_bmm16(tinv, chunks(k * beta * egc)).astype(BF16),
                qg=chunks(q * egc).astype(BF16),
                kdt=jnp.swapaxes(k3 * jnp.exp(glast - gc3), 1, 2).astype(BF16),
                gl=jnp.exp(glast)))

    for t in range(n_ch):
        for ch in chains:
            n = ch['order'][t]
            sb = ch['s'].astype(BF16)
            v_new = ch['u'][n] - _dot(ch['w'][n], sb)
            vb = v_new.astype(BF16)
            ch['outs'][n] = _dot(ch['qg'][n], sb) + _dot(ch['attn'][n], vb)
            ch['s'] = ch['s'] * ch['gl'][n] + _dot(ch['kdt'][n], vb)

    for ch in chains:
        if write_state:
            sout_ref[ch['direction'], ch['hh']] = ch['s']
    for hh in range(hps):
        cs = slice(hh * LANES, (hh + 1) * LANES)
        fwd, bwd = chains[2 * hh], chains[2 * hh + 1]
        o = jnp.concatenate([a + b for a, b in zip(fwd['outs'], bwd['outs'])], axis=0)
        o_ref[:, cs] = (_rms(o, nw_ref[...]) * _silu(ga_ref[:, cs])).astype(o_ref.dtype)


def _delta_call(proj, ab, conv_t, alog, dtb, nw, s0, layer, *, t_len, nseq, blk0, write_state, hps=2):
    has_s0 = s0 is not None
    w_a = H_A * LANES
    wh = hps * LANES
    nhb = H_A // hps

    def pspec(part):
        return pl.BlockSpec((t_len, wh), lambda b, h: (blk0 + b, part * nhb + h))

    def cspec(part):
        return pl.BlockSpec((3, wh), lambda b, h: (0, part * nhb + h))

    row1 = pl.BlockSpec((1, LANES), lambda b, h: (0, 0))
    in_specs = [pspec(0), pspec(1), pspec(2), pspec(3),
                pl.BlockSpec((t_len, LANES), lambda b, h: (blk0 + b, 0)),
                cspec(0), cspec(1), cspec(2), row1, row1, row1]
    args = [proj, proj, proj, proj, ab, conv_t, conv_t, conv_t, alog, dtb, nw]
    if has_s0:
        in_specs.append(pl.BlockSpec((None, None, 2, hps, LANES, LANES), lambda b, h: (b, layer, 0, h, 0, 0)))
        args.append(s0)
    out_specs = [pl.BlockSpec((t_len, wh), lambda b, h: (b, h))]
    out_shape = [jax.ShapeDtypeStruct((nseq * t_len, w_a), BF16)]
    if write_state:
        out_specs.append(pl.BlockSpec((None, 2, hps, LANES, LANES), lambda b, h: (b, 0, h, 0, 0)))
        out_shape.append(jax.ShapeDtypeStruct((nseq, 2, H_A, LANES, LANES), F32))

    return pl.pallas_call(
        functools.partial(_delta_kernel, t_len=t_len, hps=hps, has_s0=has_s0, write_state=write_state),
        grid=(nseq, nhb),
        in_specs=in_specs,
        out_specs=out_specs,
        out_shape=out_shape,
        compiler_params=_cparams("parallel", "parallel"),
        name=f"delta_T{t_len}",
    )(*args)


def _sgu_kernel(z_ref, nw_ref, w_ref, b_ref, o_ref, *, t_len):
    wb = o_ref.shape[1]
    z = _gelu_tanh(z_ref[...])
    u, v = z[:, :wb], z[:, wb:]
    mu = jnp.mean(v, axis=-1, keepdims=True)
    vc = v - mu
    vn = (vc * lax.rsqrt(jnp.mean(vc * vc, axis=-1, keepdims=True) + 1e-5) * nw_ref[...]).astype(BF16)
    for n in range(t_len // SGU_CHUNK):
        rs = slice(n * SGU_CHUNK, (n + 1) * SGU_CHUNK)
        for g in range(wb // LANES):
            cs = slice(g * LANES, (g + 1) * LANES)
            sp = _dot(w_ref[g], vn[rs, cs]) + b_ref[g]
            o_ref[rs, cs] = (u[rs, cs] * sp).astype(o_ref.dtype)


def _sconv_kernel(c_ref, w_ref, o_ref, *, t_len):
    wc = o_ref.shape[1]
    row = lax.broadcasted_iota(jnp.int32, (t_len, wc), 0)
    x = c_ref[...]
    y = _conv3(x[:, wc:2 * wc] * x[:, 2 * wc:], w_ref[...], row, t_len)
    o_ref[...] = (x[:, :wc] * y).astype(o_ref.dtype)


def _pool_kernel(p_ref, w_ref, sc_ref, o_ref, *, t_len):
    row = lax.broadcasted_iota(jnp.int32, (t_len, LANES), 0)
    for j, win in enumerate(POOL_WINDOWS):
        cs = slice(j * LANES, (j + 1) * LANES)
        x = p_ref[:, cs]
        half = win // 2
        acc = x
        for d in range(-half, win - half):
            if d != 0:
                acc = acc + _shift_rows(x, d, row, t_len)
        cnt = (jnp.minimum(row + (win - half), t_len) - jnp.maximum(row - half, 0)).astype(F32)
        pooled = acc / cnt - x
        o_ref[:, cs] = (_dot(pooled.astype(BF16), w_ref[j]) * sc_ref[:, cs]).astype(o_ref.dtype)


def _seq_call(kern, name, proj, col_block, width_in, params, pspecs, *, t_len, nseq, blk0, width_out):
    return pl.pallas_call(
        functools.partial(kern, t_len=t_len),
        grid=(nseq,),
        in_specs=[pl.BlockSpec((t_len, width_in), lambda b: (blk0 + b, col_block))] + pspecs,
        out_specs=pl.BlockSpec((t_len, width_out), lambda b: (b, 0)),
        out_shape=jax.ShapeDtypeStruct((nseq * t_len, width_out), BF16),
        compiler_params=_cparams("parallel"),
        name=f"{name}_T{t_len}",
    )(proj, *params)


def _outproj_kernel(x_ref, *refs, n_ctx_tiles):
    n_mix = (len(refs) - 3) // 2
    ctx_refs, lat_refs = refs[:n_mix], refs[n_mix:2 * n_mix]
    w_ref, mod_ref, o_ref = refs[2 * n_mix:]

    def run(mix_refs):
        acc = None
        for i, m_ref in enumerate(mix_refs):
            wq = m_ref.shape[1]
            part = _dot(m_ref[...], w_ref[i * wq:(i + 1) * wq, :])
            acc = part if acc is None else acc + part
        o_ref[...] = x_ref[...] + mod_ref[2:3, :] * acc

    is_ctx = pl.program_id(0) < n_ctx_tiles
    pl.when(is_ctx)(lambda: run(ctx_refs))
    pl.when(jnp.logical_not(is_ctx))(lambda: run(lat_refs))


def _outproj_call(x, mixes_ctx, mixes_lat, w_out, layer, mod, mc, sd, tm):
    m, d = x.shape
    wq = mixes_ctx[0].shape[1]
    nct = mc // tm
    midx = functools.partial(_mod_row_index, tm=tm, mc=mc, sd=sd)
    cspec = pl.BlockSpec((tm, wq), lambda i: (jnp.minimum(i, nct - 1), 0))
    lspec = pl.BlockSpec((tm, wq), lambda i: (jnp.maximum(i - nct, 0), 0))
    n_mix = len(mixes_ctx)
    return pl.pallas_call(
        functools.partial(_outproj_kernel, n_ctx_tiles=nct),
        grid=(m // tm,),
        in_specs=[pl.BlockSpec((tm, d), lambda i: (i, 0))] + [cspec] * n_mix + [lspec] * n_mix
                 + [pl.BlockSpec((None, d, d), lambda i: (layer, 0, 0)),
                    pl.BlockSpec((None, 6, d), lambda i: (midx(i), 0, 0))],
        out_specs=pl.BlockSpec((tm, d), lambda i: (i, 0)),
        out_shape=jax.ShapeDtypeStruct((m, d), F32),
        compiler_params=_cparams("parallel"),
        name="out_proj",
    )(x, *mixes_ctx, *mixes_lat, w_out, mod)


def _ffn_kernel(x_ref, nw_ref, mod_ref, wg_ref, wu_ref, wd_ref, o_ref, h_ref, acc_ref):
    j = pl.program_id(1)

    @pl.when(j == 0)
    def _():
        h = _rms(x_ref[...], nw_ref[...]) * (1.0 + mod_ref[4:5, :]) + mod_ref[3:4, :]
        h_ref[...] = h.astype(BF16)
        acc_ref[...] = jnp.zeros_like(acc_ref)

    h = h_ref[...]
    a = (_silu(_dot(h, wg_ref[...])) * _dot(h, wu_ref[...])).astype(BF16)
    acc_ref[...] += _dot(a, wd_ref[...])

    @pl.when(j == pl.num_programs(1) - 1)
    def _():
        o_ref[...] = x_ref[...] + mod_ref[5:6, :] * acc_ref[...]


def _ffn_call(x, nw, mod, wg, wu, wd, layer, mc, sd, tm, tf):
    m, d = x.shape
    f = wg.shape[2]
    midx = functools.partial(_mod_row_index, tm=tm, mc=mc, sd=sd)
    return pl.pallas_call(
        _ffn_kernel,
        grid=(m // tm, f // tf),
        in_specs=[pl.BlockSpec((tm, d), lambda i, j: (i, 0)),
                  pl.BlockSpec((1, d), lambda i, j: (0, 0)),
                  pl.BlockSpec((None, 6, d), lambda i, j: (midx(i), 0, 0)),
                  pl.BlockSpec((None, d, tf), lambda i, j: (layer, 0, j)),
                  pl.BlockSpec((None, d, tf), lambda i, j: (layer, 0, j)),
                  pl.BlockSpec((None, tf, d), lambda i, j: (layer, j, 0))],
        out_specs=pl.BlockSpec((tm, d), lambda i, j: (i, 0)),
        out_shape=jax.ShapeDtypeStruct((m, d), F32),
        scratch_shapes=[pltpu.VMEM((tm, d), BF16), pltpu.VMEM((tm, d), F32)],
        compiler_params=_cparams("parallel", "arbitrary"),
        name="ffn_dense",
    )(x, nw, mod, wg, wu, wd)


def _route_kernel(x_ref, nw_ref, mod_ref, rw_ref, h_ref, r_ref, *, n_experts):
    tm = x_ref.shape[0]
    h = _rms(x_ref[...], nw_ref[...]) * (1.0 + mod_ref[4:5, :]) + mod_ref[3:4, :]
    for cb in range(ROW_CHUNKS):
        h_ref[pl.ds(cb, tm, stride=ROW_CHUNKS), :] = h[:, cb * LANES:(cb + 1) * LANES]
    lane = lax.broadcasted_iota(jnp.int32, (tm, LANES), 1).astype(F32)
    neg = jnp.float32(-jnp.inf)
    logits = jnp.where(lane < n_experts, _dot3(h, rw_ref[...]), neg)
    m1 = jnp.max(logits, axis=-1, keepdims=True)
    i1 = jnp.min(jnp.where(logits == m1, lane, float(LANES)), axis=-1, keepdims=True)
    rest = jnp.where(lane == i1, neg, logits)
    m2 = jnp.max(rest, axis=-1, keepdims=True)
    i2 = jnp.min(jnp.where(rest == m2, lane, float(LANES)), axis=-1, keepdims=True)
    e2 = jnp.exp(m2 - m1)
    den = 1.0 + e2
    r_ref[...] = jnp.where(lane == 0.0, i1,
                           jnp.where(lane == 1.0, i2,
                                     jnp.where(lane == 2.0, 1.0 / den, jnp.where(lane == 3.0, e2 / den, 0.0))))


def _route_call(x, nw, mod, rw, mc, sd, tm, n_experts):
    m, d = x.shape
    midx = functools.partial(_mod_row_index, tm=tm, mc=mc, sd=sd)
    return pl.pallas_call(
        functools.partial(_route_kernel, n_experts=n_experts),
        grid=(m // tm,),
        in_specs=[pl.BlockSpec((tm, d), lambda i: (i, 0)),
                  pl.BlockSpec((1, d), lambda i: (0, 0)),
                  pl.BlockSpec((None, 6, d), lambda i: (midx(i), 0, 0)),
                  pl.BlockSpec((d, LANES), lambda i: (0, 0))],
        out_specs=[pl.BlockSpec((tm * ROW_CHUNKS, LANES), lambda i: (i, 0)),
                   pl.BlockSpec((tm, LANES), lambda i: (i, 0))],
        out_shape=[jax.ShapeDtypeStruct((m * ROW_CHUNKS, LANES), F32),
                   jax.ShapeDtypeStruct((m, LANES), F32)],
        compiler_params=_cparams("parallel"),
        name="moe_route",
    )(x, nw, mod, rw)


def _row_copy(src_hbm, dst_ref, src_row, dst_row, sem):
    s0 = pl.multiple_of(src_row * ROW_CHUNKS, ROW_CHUNKS)
    d0 = pl.multiple_of(dst_row * ROW_PITCH, 8)
    return pltpu.make_async_copy(src_hbm.at[pl.ds(s0, ROW_CHUNKS), :], dst_ref.at[pl.ds(d0, ROW_CHUNKS), :], sem)


def _gather_kernel(nv_ref, idx_ref, src_hbm, o_ref, buf_ref, sem, *, rows):
    blk = pl.program_id(0)
    n_valid = nv_ref[blk]

    @pl.when(blk == 0)
    def _():
        buf_ref[...] = jnp.zeros_like(buf_ref)

    def start(r, carry):
        _row_copy(src_hbm, buf_ref, idx_ref[0, r], r, sem).start()
        return carry

    def wait(r, carry):
        _row_copy(src_hbm, buf_ref, idx_ref[0, r], r, sem).wait()
        return carry

    lax.fori_loop(0, n_valid, start, 0)
    lax.fori_loop(0, n_valid, wait, 0)
    keep = lax.broadcasted_iota(jnp.int32, (rows, LANES), 0) < n_valid
    for cb in range(ROW_CHUNKS):
        val = buf_ref[pl.ds(cb, rows, stride=ROW_PITCH), :]
        o_ref[:, cb * LANES:(cb + 1) * LANES] = jnp.where(keep, val, 0.0).astype(o_ref.dtype)


def _gather_call(h_rows, src_idx, n_valid, rows):
    p = src_idx.shape[0]
    d = ROW_CHUNKS * LANES
    return pl.pallas_call(
        functools.partial(_gather_kernel, rows=rows),
        grid_spec=pltpu.PrefetchScalarGridSpec(
            num_scalar_prefetch=1,
            grid=(p // rows,),
            in_specs=[pl.BlockSpec((None, 1, rows), lambda i, n: (i, 0, 0), memory_space=pltpu.SMEM),
                      pl.BlockSpec(memory_space=pl.ANY)],
            out_specs=pl.BlockSpec((rows, d), lambda i, n: (i, 0)),
            scratch_shapes=[pltpu.VMEM((rows * ROW_PITCH, LANES), F32), pltpu.SemaphoreType.DMA(())]),
        out_shape=jax.ShapeDtypeStruct((p, d), BF16),
        compiler_params=_cparams("arbitrary"),
        name="moe_gather",
    )(n_valid, src_idx.reshape(p // rows, 1, rows), h_rows)


def _experts_kernel(te_ref, nt_ref, x_ref, wg_ref, wu_ref, wd_ref, o_ref, acc_ref):
    i, j = pl.program_id(0), pl.program_id(1)
    tm = x_ref.shape[0]
    last_j = pl.num_programs(1) - 1

    @pl.when(i < nt_ref[0])
    def _():
        x = x_ref[...]
        a = (_silu(_dot(x, wg_ref[...].astype(BF16))) * _dot(x, wu_ref[...].astype(BF16))).astype(BF16)
        part = _dot(a, wd_ref[...].astype(BF16))

        @pl.when(j == 0)
        def _():
            acc_ref[...] = part

        @pl.when(j != 0)
        def _():
            acc_ref[...] += part

        @pl.when(j == last_j)
        def _():
            for cb in range(ROW_CHUNKS):
                o_ref[pl.ds(cb, tm, stride=ROW_CHUNKS), :] = acc_ref[:, cb * LANES:(cb + 1) * LANES]

    @pl.when((i >= nt_ref[0]) & (j == last_j))
    def _():
        o_ref[...] = jnp.zeros_like(o_ref)


def _experts_call(tile_e, ntiles, xs, wg, wu, wd, layer, tm, tf):
    d, f = wg.shape[2:]
    p = xs.shape[0]
    nf = f // tf

    def row_blk(i, j, te, nt):
        return (jnp.minimum(i, nt[0] - 1), 0)

    def jj(i, j, nt):
        return jnp.where(i < nt[0], j, nf - 1)

    def w_in_blk(i, j, te, nt):
        return (layer, te[i], 0, jj(i, j, nt))

    def w_out_blk(i, j, te, nt):
        return (layer, te[i], jj(i, j, nt), 0)

    return pl.pallas_call(
        _experts_kernel,
        grid_spec=pltpu.PrefetchScalarGridSpec(
            num_scalar_prefetch=2,
            grid=(p // tm, nf),
            in_specs=[pl.BlockSpec((tm, d), row_blk),
                      pl.BlockSpec((None, None, d, tf), w_in_blk),
                      pl.BlockSpec((None, None, d, tf), w_in_blk),
                      pl.BlockSpec((None, None, tf, d), w_out_blk)],
            out_specs=pl.BlockSpec((tm * ROW_CHUNKS, LANES), lambda i, j, te, nt: (i, 0)),
            scratch_shapes=[pltpu.VMEM((tm, d), F32)]),
        out_shape=jax.ShapeDtypeStruct((p * ROW_CHUNKS, LANES), F32),
        compiler_params=_cparams("arbitrary", "arbitrary"),
        name="moe_experts",
    )(tile_e, ntiles, xs, wg, wu, wd)


def _combine_kernel(idx_ref, x_ref, r_ref, mod_ref, ys_hbm, o_ref, buf_ref, sem):
    tc = x_ref.shape[0]
    n_rows = TOP_K * tc

    def start(r, carry):
        _row_copy(ys_hbm, buf_ref, idx_ref[0, r], r, sem).start()
        return carry

    def wait(r, carry):
        _row_copy(ys_hbm, buf_ref, idx_ref[0, r], r, sem).wait()
        return carry

    lax.fori_loop(0, n_rows, start, 0, unroll=8)
    lax.fori_loop(0, n_rows, wait, 0, unroll=8)
    w1 = r_ref[:, 2:3]
    w2 = r_ref[:, 3:4]
    for cb in range(ROW_CHUNKS):
        cs = slice(cb * LANES, (cb + 1) * LANES)
        y1 = buf_ref[pl.ds(cb, tc, stride=ROW_PITCH), :]
        y2 = buf_ref[pl.ds(tc * ROW_PITCH + cb, tc, stride=ROW_PITCH), :]
        o_ref[:, cs] = x_ref[:, cs] + mod_ref[5:6, cs] * (y1 * w1 + y2 * w2)


def _combine_call(dest, x, route, mod, ys, mc, sd, tc):
    m, d = x.shape
    midx = functools.partial(_mod_row_index, tm=tc, mc=mc, sd=sd)
    return pl.pallas_call(
        _combine_kernel,
        grid=(m // tc,),
        in_specs=[pl.BlockSpec((None, 1, TOP_K * tc), lambda i: (i, 0, 0), memory_space=pltpu.SMEM),
                  pl.BlockSpec((tc, d), lambda i: (i, 0)),
                  pl.BlockSpec((tc, LANES), lambda i: (i, 0)),
                  pl.BlockSpec((None, 6, d), lambda i: (midx(i), 0, 0)),
                  pl.BlockSpec(memory_space=pl.ANY)],
        out_specs=pl.BlockSpec((tc, d), lambda i: (i, 0)),
        out_shape=jax.ShapeDtypeStruct((m, d), F32),
        scratch_shapes=[pltpu.VMEM((TOP_K * tc * ROW_PITCH, LANES), F32), pltpu.SemaphoreType.DMA(())],
        compiler_params=_cparams("arbitrary"),
        name="moe_combine",
    )(dest, x, route, mod, ys)


def _moe_plan(route, n_experts, tm, n_tiles, tc, gather_rows):
    m = route.shape[0]
    e_flat = jnp.concatenate([route[:, 0], route[:, 1]]).astype(jnp.int32)
    onehot = (e_flat[:, None] == jnp.arange(n_experts, dtype=jnp.int32)[None, :]).astype(jnp.int32)
    csum = jnp.cumsum(onehot, axis=0)
    cnt = csum[-1]
    rank = jnp.take_along_axis(csum, e_flat[:, None], axis=1)[:, 0] - 1
    gsz = ((cnt + tm - 1) // tm) * tm
    off_end = jnp.cumsum(gsz)
    off = off_end - gsz
    dest = off[e_flat] + rank
    n_used = off_end[-1] // tm

    def owner(starts):
        return jnp.minimum(jnp.sum((off_end[None, :] <= starts[:, None]).astype(jnp.int32), axis=1), n_experts - 1)

    tiles = jnp.arange(n_tiles, dtype=jnp.int32)
    tile_e = owner(tiles * tm)
    tile_e = jnp.where(tiles < n_used, tile_e, tile_e[jnp.maximum(n_used - 1, 0)])
    blocks = jnp.arange(n_tiles * tm // gather_rows, dtype=jnp.int32) * gather_rows
    n_valid = jnp.where(blocks < off_end[-1], jnp.clip((off + cnt)[owner(blocks)] - blocks, 0, gather_rows), 0)
    tok = jnp.tile(jnp.arange(m, dtype=jnp.int32), TOP_K)
    src = jnp.zeros((n_tiles * tm,), jnp.int32).at[dest].set(tok)
    dest_tiles = jnp.concatenate([dest[:m].reshape(m // tc, 1, tc), dest[m:].reshape(m // tc, 1, tc)], axis=2)
    i32 = lambda a: a.astype(jnp.int32)
    return src, dest_tiles, i32(tile_e), i32(n_used.reshape(1)), i32(n_valid)


def _moe_layer(x, nw, mod, rw, wg, wu, wd, layer, mc, sd, tm_route, tm, tf, tc, gather_rows):
    m, d = x.shape
    n_experts = wg.shape[1]
    assert tm % gather_rows == 0 and (TOP_K * m) % tm == 0
    n_tiles = (TOP_K * m) // tm + n_experts
    h_rows, route = _route_call(x, nw, mod, rw, mc, sd, tm_route, n_experts)
    src, dest_tiles, tile_e, n_used, n_valid = _moe_plan(route, n_experts, tm, n_tiles, tc, gather_rows)
    xs = _gather_call(h_rows, src, n_valid, gather_rows)
    ys = _experts_call(tile_e, n_used, xs, wg, wu, wd, layer, tm, tf)
    return _combine_call(dest_tiles, x, route, mod, ys, mc, sd, tc)


def _final_kernel(x_ref, w_ref, o_ref):
    o_ref[...] = _rms(x_ref[...], w_ref[...])


def _final_call(x, w, tm):
    m, d = x.shape
    return pl.pallas_call(
        _final_kernel,
        grid=(m // tm,),
        in_specs=[pl.BlockSpec((tm, d), lambda i: (i, 0)), pl.BlockSpec((1, d), lambda i: (0, 0))],
        out_specs=pl.BlockSpec((tm, d), lambda i: (i, 0)),
        out_shape=jax.ShapeDtypeStruct((m, d), F32),
        compiler_params=_cparams("parallel"),
        name="final_norm",
    )(x, w)


def _sincos_2d(rows, cols, dim):
    quarter = dim // 4
    omega = 1.0 / (10000.0 ** (jnp.arange(quarter, dtype=F32) / quarter))
    r = jnp.arange(rows, dtype=F32)[:, None] * omega
    cc = jnp.arange(cols, dtype=F32)[:, None] * omega
    r_emb = jnp.concatenate([jnp.sin(r), jnp.cos(r)], axis=-1)
    c_emb = jnp.concatenate([jnp.sin(cc), jnp.cos(cc)], axis=-1)
    emb = jnp.concatenate([jnp.broadcast_to(r_emb[:, None, :], (rows, cols, dim // 2)),
                           jnp.broadcast_to(c_emb[None, :, :], (rows, cols, dim // 2))], axis=-1)
    return emb.reshape(rows * cols, dim)


def _pad_lanes(a):
    return jnp.pad(a, ((0, 0), (0, LANES - a.shape[1])))


def kernel(x_prompt, x_sample, c, state_delta, c_ctx, norm1_w, norm2_w, w_mod, b_mod, w_in, w_out, qkv_conv_w, delta_a_log, delta_dt_bias, delta_norm_w, sgu_norm_w, sgu_w, sgu_b, sconv_w, pool_w, pool_scale, ffn_w_gate, ffn_w_up, ffn_w_down, router_w, moe_w_gate, moe_w_up, moe_w_down, final_norm_w):
    bc, sc, d = x_prompt.shape
    bd, sd, _ = x_sample.shape
    depth = w_in.shape[0]
    mc, ml = bc * sc, bd * sd
    m = mc + ml
    w_a = H_A * LANES
    w_b, w_c, w_d = sgu_norm_w.shape[1], sconv_w.shape[1], pool_scale.shape[1]
    assert d == ROW_CHUNKS * LANES and mc % sd == 0 and w_a == w_b == w_c == w_d == d // 4
    assert delta_a_log.shape[1:] == (2, H_A) and delta_norm_w.shape[1] == LANES

    tm_mm = 512 if m % 512 == 0 else 256
    grid_w = 64
    pos = _sincos_2d(sd // grid_w, grid_w, d)
    x = jnp.concatenate([x_prompt.reshape(mc, d), (x_sample + pos[None]).reshape(ml, d)], axis=0)

    rm = -(-(1 + bd) // 8) * 8
    cond = jnp.zeros((rm, d), F32).at[0].set(c_ctx).at[1:1 + bd].set(c)
    mod_all = _mod_call(cond, w_mod, b_mod).reshape(depth, rm, 6, d)

    o0 = 3 * w_a
    o1 = o0 + w_a
    o3 = o1 + 4 * H_A
    o4 = o3 + 2 * w_b
    o5 = o4 + 3 * w_c
    blk0 = mc // sd
    w_main_all = jnp.concatenate([w_in[:, :, :o1], w_in[:, :, o3:]], axis=2).astype(BF16)
    w_ab_all = jnp.pad(w_in[:, :, o1:o3], ((0, 0), (0, 0), (0, LANES - (o3 - o1)))).astype(BF16)
    w_out_b = w_out.astype(BF16)
    ffn_b = [w.astype(BF16) for w in (ffn_w_gate, ffn_w_up, ffn_w_down)]
    moe_w = (moe_w_gate, moe_w_up, moe_w_down.astype(BF16))
    ctx_states = []
    for l in range(depth):
        mod = mod_all[l]
        tm_in = 1024 if (mc % 1024 == 0 and sd % 1024 == 0) else tm_mm
        proj, ab = _inproj_call(x, norm1_w[l][None], mod, w_main_all, w_ab_all, l, mc, sd, tm_in, 1024)

        conv_t = qkv_conv_w[l].T
        alog = _pad_lanes(delta_a_log[l].reshape(1, 2 * H_A))
        dtb = _pad_lanes(delta_dt_bias[l].reshape(1, 2 * H_A))
        nwa = delta_norm_w[l][None]
        a_ctx, s_ctx = _delta_call(proj, ab, conv_t, alog, dtb, nwa, None, l,
                                   t_len=sc, nseq=bc, blk0=0, write_state=True)
        (a_lat,) = _delta_call(proj, ab, conv_t, alog, dtb, nwa, state_delta, l,
                               t_len=sd, nseq=bd, blk0=blk0, write_state=False)
        ctx_states.append(s_ctx)

        def both_paths(kern, name, col_block, width_in, params, pspecs, width_out):
            return (_seq_call(kern, name, proj, col_block, width_in, params, pspecs,
                              t_len=sc, nseq=bc, blk0=0, width_out=width_out),
                    _seq_call(kern, name, proj, col_block, width_in, params, pspecs,
                              t_len=sd, nseq=bd, blk0=blk0, width_out=width_out))

        n_g = sgu_w.shape[1]
        b_ctx, b_lat = both_paths(
            _sgu_kernel, "sgu", o1 // (2 * w_b), 2 * w_b,
            [sgu_norm_w[l][None], sgu_w[l].astype(BF16), sgu_b[l].reshape(n_g, SGU_CHUNK, 1)],
            [pl.BlockSpec((1, w_b), lambda b: (0, 0)),
             pl.BlockSpec((n_g, SGU_CHUNK, SGU_CHUNK), lambda b: (0, 0, 0)),
             pl.BlockSpec((n_g, SGU_CHUNK, 1), lambda b: (0, 0, 0))], w_b)
        c_ctx_mix, c_lat = both_paths(
            _sconv_kernel, "sconv", (o1 + 2 * w_b) // (3 * w_c), 3 * w_c,
            [sconv_w[l].T], [pl.BlockSpec((3, w_c), lambda b: (0, 0))], w_c)
        n_gd = pool_w.shape[1]
        d_ctx, d_lat = both_paths(
            _pool_kernel, "pool", (o1 + 2 * w_b + 3 * w_c) // w_d, w_d,
            [pool_w[l].astype(BF16), pool_scale[l][None]],
            [pl.BlockSpec((n_gd, LANES, LANES), lambda b: (0, 0, 0)),
             pl.BlockSpec((1, w_d), lambda b: (0, 0))], w_d)

        x = _outproj_call(x, (a_ctx, b_ctx, c_ctx_mix, d_ctx), (a_lat, b_lat, c_lat, d_lat),
                          w_out_b, l, mod, mc, sd, tm_mm)

        jl = l // 2
        if l % 2 == 0:
            x = _ffn_call(x, norm2_w[l][None], mod, *ffn_b, jl, mc, sd, tm_mm, 512)
        else:
            x = _moe_layer(x, norm2_w[l][None], mod, _pad_lanes(router_w[jl]), *moe_w, jl,
                           mc, sd, tm_route=256, tm=512, tf=512, tc=256, gather_rows=512)

    y = _final_call(x, final_norm_w[None], tm_mm)
    y_prompt = y[:mc].reshape(bc, sc, d)
    y_sample = y[mc:].reshape(bd, sd, d)
    new_state = jnp.stack(ctx_states, axis=1).astype(x_prompt.dtype)
    return (y_prompt, y_sample, new_state)
```

```python
import functools

import jax
import jax.numpy as jnp
from jax import lax
from jax.experimental import pallas as pl
from jax.experimental.pallas import tpu as pltpu

F32 = jnp.float32
BF16 = jnp.bfloat16

LANES = 128
ROW_CHUNKS = 16
ROW_PITCH = 24
DELTA_CHUNK = 64
SGU_CHUNK = 128
H_A = 4
POOL_WINDOWS = (2, 4, 8, 16)
TOP_K = 2
VMEM_LIMIT = 56 * 1024 * 1024


def _cparams(*sem):
    return pltpu.CompilerParams(dimension_semantics=sem, vmem_limit_bytes=VMEM_LIMIT)


def _silu(x):
    return x / (1.0 + jnp.exp(-x))


def _sigmoid(x):
    return 1.0 / (1.0 + jnp.exp(-x))


def _softplus(x):
    return jnp.maximum(x, 0.0) + jnp.log1p(jnp.exp(-jnp.abs(x)))


def _gelu_tanh(x):
    return 0.5 * x * (1.0 + jnp.tanh(0.7978845608028654 * (x + 0.044715 * (x * x * x))))


def _rms(x, w, eps=1e-6):
    return x * lax.rsqrt(jnp.mean(x * x, axis=-1, keepdims=True) + eps) * w


def _dot(a, b):
    return jnp.dot(a, b, preferred_element_type=F32)


def _split(a):
    hi = a.astype(BF16)
    lo = (a - hi.astype(F32)).astype(BF16)
    return hi, lo


def _dot3(a, b):
    a_hi, a_lo = _split(a)
    b_hi, b_lo = _split(b)
    return _dot(a_hi, b_hi) + _dot(a_hi, b_lo) + _dot(a_lo, b_hi)


def _bmm(a, b):
    return jnp.einsum('nij,njk->nik', a, b, preferred_element_type=F32)


def _bmm16(a, b):
    return _bmm(a.astype(BF16), b.astype(BF16))


def _mod_kernel(c_ref, w_ref, b_ref, o_ref):
    a = _silu(c_ref[...]).astype(BF16)
    o_ref[...] = _dot(a, w_ref[...].astype(BF16)) + b_ref[...]


def _mod_call(cond, w_mod, b_mod, tn=1024):
    depth, d, n = w_mod.shape
    rm = cond.shape[0]
    return pl.pallas_call(
        _mod_kernel,
        grid=(depth, n // tn),
        in_specs=[pl.BlockSpec((rm, d), lambda l, j: (0, 0)),
                  pl.BlockSpec((None, d, tn), lambda l, j: (l, 0, j)),
                  pl.BlockSpec((None, 1, tn), lambda l, j: (l, 0, j))],
        out_specs=pl.BlockSpec((None, rm, tn), lambda l, j: (l, 0, j)),
        out_shape=jax.ShapeDtypeStruct((depth, rm, n), F32),
        compiler_params=_cparams("parallel", "parallel"),
        name="adaln_mod",
    )(cond, w_mod, b_mod.reshape(depth, 1, n))


def _mod_row_index(i, tm, mc, sd):
    r0 = i * tm
    return jnp.where(r0 < mc, 0, 1 + (r0 - mc) // sd)


def _inproj_kernel(x_ref, nw_ref, mod_ref, w_ref, wab_ref, proj_ref, ab_ref, h_ref):
    @pl.when(pl.program_id(1) == 0)
    def _():
        h = _rms(x_ref[...], nw_ref[...]) * (1.0 + mod_ref[1:2, :]) + mod_ref[0:1, :]
        hb = h.astype(BF16)
        h_ref[...] = hb
        ab_ref[...] = _dot(hb, wab_ref[...])

    proj_ref[...] = _dot(h_ref[...], w_ref[...])


def _inproj_call(x, nw, mod, w_main, w_ab, layer, mc, sd, tm, tn):
    m, d = x.shape
    n = w_main.shape[2]
    midx = functools.partial(_mod_row_index, tm=tm, mc=mc, sd=sd)
    return pl.pallas_call(
        _inproj_kernel,
        grid=(m // tm, n // tn),
        in_specs=[pl.BlockSpec((tm, d), lambda i, j: (i, 0)),
                  pl.BlockSpec((1, d), lambda i, j: (0, 0)),
                  pl.BlockSpec((None, 6, d), lambda i, j: (midx(i), 0, 0)),
                  pl.BlockSpec((None, d, tn), lambda i, j: (layer, 0, j)),
                  pl.BlockSpec((None, d, LANES), lambda i, j: (layer, 0, 0))],
        out_specs=[pl.BlockSpec((tm, tn), lambda i, j: (i, j)),
                   pl.BlockSpec((tm, LANES), lambda i, j: (i, 0))],
        out_shape=[jax.ShapeDtypeStruct((m, n), F32), jax.ShapeDtypeStruct((m, LANES), F32)],
        scratch_shapes=[pltpu.VMEM((tm, d), BF16)],
        compiler_params=_cparams("parallel", "arbitrary"),
        name="in_proj",
    )(x, nw, mod, w_main, w_ab)


def _shift_rows(x, d, row, t_len):
    if d == 0:
        return x
    y = pltpu.roll(x, (-d) % t_len, axis=0)
    ok = (row + d >= 0) & (row + d < t_len)
    return jnp.where(ok, y, 0.0)


def _conv3(x, w, row, t_len):
    return (_shift_rows(x, -1, row, t_len) * w[0:1, :] + x * w[1:2, :]
            + _shift_rows(x, 1, row, t_len) * w[2:3, :])


def _chunk_cumsum(g, row, t_len, reverse):
    pos = row % DELTA_CHUNK
    s = 1
    while s < DELTA_CHUNK:
        if reverse:
            g = g + jnp.where(pos < DELTA_CHUNK - s, pltpu.roll(g, t_len - s, axis=0), 0.0)
        else:
            g = g + jnp.where(pos >= s, pltpu.roll(g, s, axis=0), 0.0)
        s *= 2
    return g


def _unit_tri_inverse(lmat, r, c):
    eye = jnp.where(r == c, 1.0, 0.0)

    def same_block(b):
        return (r // b) == (c // b)

    x = jnp.where(same_block(8), -lmat, 0.0)
    x2 = _bmm16(x, x)
    x4 = _bmm16(x2, x2)
    p = _bmm16(_bmm16(eye + x, eye + x2), eye + x4)
    b = 8
    while b < DELTA_CHUNK:
        off = jnp.where(same_block(2 * b) & jnp.logical_not(same_block(b)), lmat, 0.0)
        pb = p.astype(BF16)
        p = p - _bmm16(_bmm16(pb, off), pb)
        b *= 2
    return p


def _delta_kernel(*refs, t_len, hps, has_s0, write_state):
    (q_ref, k_ref, v_ref, ga_ref, ab_ref, cq_ref, ck_ref, cv_ref, alog_ref, dtb_ref, nw_ref), rest = refs[:11], refs[11:]
    if has_s0:
        s0_ref, rest = rest[0], rest[1:]
    o_ref, rest = rest[0], rest[1:]
    if write_state:
        sout_ref = rest[0]

    n_ch = t_len // DELTA_CHUNK
    cl = DELTA_CHUNK
    row = lax.broadcasted_iota(jnp.int32, (t_len, LANES), 0)
    lane = lax.broadcasted_iota(jnp.int32, (t_len, LANES), 1)
    r64 = lax.broadcasted_iota(jnp.int32, (cl, cl), 0)
    c64 = lax.broadcasted_iota(jnp.int32, (cl, cl), 1)

    def l2n(x):
        return x * lax.rsqrt(jnp.sum(x * x, axis=-1, keepdims=True) + 1e-6)

    def column(a, idx):
        col = jnp.sum(jnp.where(lane == idx, a, 0.0), axis=1, keepdims=True)
        return jnp.broadcast_to(col, (t_len, LANES))

    def chunks(a):
        return a.reshape(n_ch, cl, LANES)

    ab = ab_ref[...]
    g_all = -jnp.exp(alog_ref[...]) * _softplus(ab + dtb_ref[...])
    b_all = _sigmoid(ab)

    chains = []
    for hh in range(hps):
        head = pl.program_id(1) * hps + hh
        cs = slice(hh * LANES, (hh + 1) * LANES)
        q = l2n(_silu(_conv3(q_ref[:, cs], cq_ref[:, cs], row, t_len))) * (LANES ** -0.5)
        k = l2n(_silu(_conv3(k_ref[:, cs], ck_ref[:, cs], row, t_len)))
        v = _silu(_conv3(v_ref[:, cs], cv_ref[:, cs], row, t_len))
        k3 = chunks(k)
        kb16 = k3.astype(BF16)
        kk = jnp.einsum('ncd,nsd->ncs', kb16, kb16, preferred_element_type=F32)
        qk = jnp.einsum('ncd,nsd->ncs', chunks(q).astype(BF16), kb16, preferred_element_type=F32)
        for direction in range(2):
            rev = direction == 1
            g = column(g_all, direction * H_A + head)
            beta = column(b_all, 2 * H_A + direction * H_A + head)
            gc = _chunk_cumsum(g, row, t_len, rev)
            gc3 = chunks(gc)
            gc_rows = jnp.swapaxes(gc3, 1, 2)[:, :cl, :]
            dmat = gc3[:, :, :cl] - gc_rows
            if rev:
                incl, strict = c64 >= r64, c64 > r64
            else:
                incl, strict = c64 <= r64, c64 < r64
            decay = jnp.exp(jnp.where(incl, dmat, 0.0))
            lmat = jnp.where(strict, chunks(beta)[:, :, :cl] * kk * decay, 0.0)
            tinv = _unit_tri_inverse(lmat, r64, c64).astype(BF16)
            egc = jnp.exp(gc)
            last = 0 if rev else cl - 1
            glast = gc3[:, last:last + 1, :]
            if has_s0:
                s_init = s0_ref[direction, hh]
            else:
                s_init = jnp.zeros((LANES, LANES), F32)
            chains.append(dict(
                hh=hh, direction=direction, s=s_init, outs=[None] * n_ch,
                order=list(range(n_ch - 1, -1, -1) if rev else range(n_ch)),
                attn=jnp.where(incl, qk * decay, 0.0).astype(BF16),
                u=_bmm16(tinv, chunks(v * beta)),
                w=_bmm16(tinv, chunks(k * beta * egc)).astype(BF16),
                qg=chunks(q * egc).astype(BF16),
                kdt=jnp.swapaxes(k3 * jnp.exp(glast - gc3), 1, 2).astype(BF16),
                gl=jnp.exp(glast)))

    for t in range(n_ch):
        for ch in chains:
            n = ch['order'][t]
            sb = ch['s'].astype(BF16)
            v_new = ch['u'][n] - _dot(ch['w'][n], sb)
            vb = v_new.astype(BF16)
            ch['outs'][n] = _dot(ch['qg'][n], sb) + _dot(ch['attn'][n], vb)
            ch['s'] = ch['s'] * ch['gl'][n] + _dot(ch['kdt'][n], vb)

    for ch in chains:
        if write_state:
            sout_ref[ch['direction'], ch['hh']] = ch['s']
    for hh in range(hps):
        cs = slice(hh * LANES, (hh + 1) * LANES)
        fwd, bwd = chains[2 * hh], chains[2 * hh + 1]
        o = jnp.concatenate([a + b for a, b in zip(fwd['outs'], bwd['outs'])], axis=0)
        o_ref[:, cs] = (_rms(o, nw_ref[...]) * _silu(ga_ref[:, cs])).astype(o_ref.dtype)


def _delta_call(proj, ab, conv_t, alog, dtb, nw, s0, layer, *, t_len, nseq, blk0, write_state, hps=2):
    has_s0 = s0 is not None
    w_a = H_A * LANES
    wh = hps * LANES
    nhb = H_A // hps

    def pspec(part):
        return pl.BlockSpec((t_len, wh), lambda b, h: (blk0 + b, part * nhb + h))

    def cspec(part):
        return pl.BlockSpec((3, wh), lambda b, h: (0, part * nhb + h))

    row1 = pl.BlockSpec((1, LANES), lambda b, h: (0, 0))
    in_specs = [pspec(0), pspec(1), pspec(2), pspec(3),
                pl.BlockSpec((t_len, LANES), lambda b, h: (blk0 + b, 0)),
                cspec(0), cspec(1), cspec(2), row1, row1, row1]
    args = [proj, proj, proj, proj, ab, conv_t, conv_t, conv_t, alog, dtb, nw]
    if has_s0:
        in_specs.append(pl.BlockSpec((None, None, 2, hps, LANES, LANES), lambda b, h: (b, layer, 0, h, 0, 0)))
        args.append(s0)
    out_specs = [pl.BlockSpec((t_len, wh), lambda b, h: (b, h))]
    out_shape = [jax.ShapeDtypeStruct((nseq * t_len, w_a), BF16)]
    if write_state:
        out_specs.append(pl.BlockSpec((None, 2, hps, LANES, LANES), lambda b, h: (b, 0, h, 0, 0)))
        out_shape.append(jax.ShapeDtypeStruct((nseq, 2, H_A, LANES, LANES), F32))

    return pl.pallas_call(
        functools.partial(_delta_kernel, t_len=t_len, hps=hps, has_s0=has_s0, write_state=write_state),
        grid=(nseq, nhb),
        in_specs=in_specs,
        out_specs=out_specs,
        out_shape=out_shape,
        compiler_params=_cparams("parallel", "parallel"),
        name=f"delta_T{t_len}",
    )(*args)


def _sgu_kernel(z_ref, nw_ref, w_ref, b_ref, o_ref, *, t_len):
    wb = o_ref.shape[1]
    z = _gelu_tanh(z_ref[...])
    u, v = z[:, :wb], z[:, wb:]
    mu = jnp.mean(v, axis=-1, keepdims=True)
    vc = v - mu
    vn = (vc * lax.rsqrt(jnp.mean(vc * vc, axis=-1, keepdims=True) + 1e-5) * nw_ref[...]).astype(BF16)
    for n in range(t_len // SGU_CHUNK):
        rs = slice(n * SGU_CHUNK, (n + 1) * SGU_CHUNK)
        for g in range(wb // LANES):
            cs = slice(g * LANES, (g + 1) * LANES)
            sp = _dot(w_ref[g], vn[rs, cs]) + b_ref[g]
            o_ref[rs, cs] = (u[rs, cs] * sp).astype(o_ref.dtype)


def _sconv_kernel(c_ref, w_ref, o_ref, *, t_len):
    wc = o_ref.shape[1]
    row = lax.broadcasted_iota(jnp.int32, (t_len, wc), 0)
    x = c_ref[...]
    y = _conv3(x[:, wc:2 * wc] * x[:, 2 * wc:], w_ref[...], row, t_len)
    o_ref[...] = (x[:, :wc] * y).astype(o_ref.dtype)


def _pool_kernel(p_ref, w_ref, sc_ref, o_ref, *, t_len):
    row = lax.broadcasted_iota(jnp.int32, (t_len, LANES), 0)
    for j, win in enumerate(POOL_WINDOWS):
        cs = slice(j * LANES, (j + 1) * LANES)
        x = p_ref[:, cs]
        half = win // 2
        acc = x
        for d in range(-half, win - half):
            if d != 0:
                acc = acc + _shift_rows(x, d, row, t_len)
        cnt = (jnp.minimum(row + (win - half), t_len) - jnp.maximum(row - half, 0)).astype(F32)
        pooled = acc / cnt - x
        o_ref[:, cs] = (_dot(pooled.astype(BF16), w_ref[j]) * sc_ref[:, cs]).astype(o_ref.dtype)


def _seq_call(kern, name, proj, col_block, width_in, params, pspecs, *, t_len, nseq, blk0, width_out):
    return pl.pallas_call(
        functools.partial(kern, t_len=t_len),
        grid=(nseq,),
        in_specs=[pl.BlockSpec((t_len, width_in), lambda b: (blk0 + b, col_block))] + pspecs,
        out_specs=pl.BlockSpec((t_len, width_out), lambda b: (b, 0)),
        out_shape=jax.ShapeDtypeStruct((nseq * t_len, width_out), BF16),
        compiler_params=_cparams("parallel"),
        name=f"{name}_T{t_len}",
    )(proj, *params)


def _outproj_kernel(x_ref, *refs, n_ctx_tiles):
    n_mix = (len(refs) - 3) // 2
    ctx_refs, lat_refs = refs[:n_mix], refs[n_mix:2 * n_mix]
    w_ref, mod_ref, o_ref = refs[2 * n_mix:]

    def run(mix_refs):
        acc = None
        for i, m_ref in enumerate(mix_refs):
            wq = m_ref.shape[1]
            part = _dot(m_ref[...], w_ref[i * wq:(i + 1) * wq, :])
            acc = part if acc is None else acc + part
        o_ref[...] = x_ref[...] + mod_ref[2:3, :] * acc

    is_ctx = pl.program_id(0) < n_ctx_tiles
    pl.when(is_ctx)(lambda: run(ctx_refs))
    pl.when(jnp.logical_not(is_ctx))(lambda: run(lat_refs))


def _outproj_call(x, mixes_ctx, mixes_lat, w_out, layer, mod, mc, sd, tm):
    m, d = x.shape
    wq = mixes_ctx[0].shape[1]
    nct = mc // tm
    midx = functools.partial(_mod_row_index, tm=tm, mc=mc, sd=sd)
    cspec = pl.BlockSpec((tm, wq), lambda i: (jnp.minimum(i, nct - 1), 0))
    lspec = pl.BlockSpec((tm, wq), lambda i: (jnp.maximum(i - nct, 0), 0))
    n_mix = len(mixes_ctx)
    return pl.pallas_call(
        functools.partial(_outproj_kernel, n_ctx_tiles=nct),
        grid=(m // tm,),
        in_specs=[pl.BlockSpec((tm, d), lambda i: (i, 0))] + [cspec] * n_mix + [lspec] * n_mix
                 + [pl.BlockSpec((None, d, d), lambda i: (layer, 0, 0)),
                    pl.BlockSpec((None, 6, d), lambda i: (midx(i), 0, 0))],
        out_specs=pl.BlockSpec((tm, d), lambda i: (i, 0)),
        out_shape=jax.ShapeDtypeStruct((m, d), F32),
        compiler_params=_cparams("parallel"),
        name="out_proj",
    )(x, *mixes_ctx, *mixes_lat, w_out, mod)


def _ffn_kernel(x_ref, nw_ref, mod_ref, wg_ref, wu_ref, wd_ref, o_ref, h_ref, acc_ref):
    j = pl.program_id(1)

    @pl.when(j == 0)
    def _():
        h = _rms(x_ref[...], nw_ref[...]) * (1.0 + mod_ref[4:5, :]) + mod_ref[3:4, :]
        h_ref[...] = h.astype(BF16)
        acc_ref[...] = jnp.zeros_like(acc_ref)

    h = h_ref[...]
    a = (_silu(_dot(h, wg_ref[...])) * _dot(h, wu_ref[...])).astype(BF16)
    acc_ref[...] += _dot(a, wd_ref[...])

    @pl.when(j == pl.num_programs(1) - 1)
    def _():
        o_ref[...] = x_ref[...] + mod_ref[5:6, :] * acc_ref[...]


def _ffn_call(x, nw, mod, wg, wu, wd, layer, mc, sd, tm, tf):
    m, d = x.shape
    f = wg.shape[2]
    midx = functools.partial(_mod_row_index, tm=tm, mc=mc, sd=sd)
    return pl.pallas_call(
        _ffn_kernel,
        grid=(m // tm, f // tf),
        in_specs=[pl.BlockSpec((tm, d), lambda i, j: (i, 0)),
                  pl.BlockSpec((1, d), lambda i, j: (0, 0)),
                  pl.BlockSpec((None, 6, d), lambda i, j: (midx(i), 0, 0)),
                  pl.BlockSpec((None, d, tf), lambda i, j: (layer, 0, j)),
                  pl.BlockSpec((None, d, tf), lambda i, j: (layer, 0, j)),
                  pl.BlockSpec((None, tf, d), lambda i, j: (layer, j, 0))],
        out_specs=pl.BlockSpec((tm, d), lambda i, j: (i, 0)),
        out_shape=jax.ShapeDtypeStruct((m, d), F32),
        scratch_shapes=[pltpu.VMEM((tm, d), BF16), pltpu.VMEM((tm, d), F32)],
        compiler_params=_cparams("parallel", "arbitrary"),
        name="ffn_dense",
    )(x, nw, mod, wg, wu, wd)


def _route_kernel(x_ref, nw_ref, mod_ref, rw_ref, h_ref, r_ref, *, n_experts):
    tm = x_ref.shape[0]
    h = _rms(x_ref[...], nw_ref[...]) * (1.0 + mod_ref[4:5, :]) + mod_ref[3:4, :]
    for cb in range(ROW_CHUNKS):
        h_ref[pl.ds(cb, tm, stride=ROW_CHUNKS), :] = h[:, cb * LANES:(cb + 1) * LANES]
    lane = lax.broadcasted_iota(jnp.int32, (tm, LANES), 1).astype(F32)
    neg = jnp.float32(-jnp.inf)
    logits = jnp.where(lane < n_experts, _dot3(h, rw_ref[...]), neg)
    m1 = jnp.max(logits, axis=-1, keepdims=True)
    i1 = jnp.min(jnp.where(logits == m1, lane, float(LANES)), axis=-1, keepdims=True)
    rest = jnp.where(lane == i1, neg, logits)
    m2 = jnp.max(rest, axis=-1, keepdims=True)
    i2 = jnp.min(jnp.where(rest == m2, lane, float(LANES)), axis=-1, keepdims=True)
    e2 = jnp.exp(m2 - m1)
    den = 1.0 + e2
    r_ref[...] = jnp.where(lane == 0.0, i1,
                           jnp.where(lane == 1.0, i2,
                                     jnp.where(lane == 2.0, 1.0 / den, jnp.where(lane == 3.0, e2 / den, 0.0))))


def _route_call(x, nw, mod, rw, mc, sd, tm, n_experts):
    m, d = x.shape
    midx = functools.partial(_mod_row_index, tm=tm, mc=mc, sd=sd)
    return pl.pallas_call(
        functools.partial(_route_kernel, n_experts=n_experts),
        grid=(m // tm,),
        in_specs=[pl.BlockSpec((tm, d), lambda i: (i, 0)),
                  pl.BlockSpec((1, d), lambda i: (0, 0)),
                  pl.BlockSpec((None, 6, d), lambda i: (midx(i), 0, 0)),
                  pl.BlockSpec((d, LANES), lambda i: (0, 0))],
        out_specs=[pl.BlockSpec((tm * ROW_CHUNKS, LANES), lambda i: (i, 0)),
                   pl.BlockSpec((tm, LANES), lambda i: (i, 0))],
        out_shape=[jax.ShapeDtypeStruct((m * ROW_CHUNKS, LANES), F32),
                   jax.ShapeDtypeStruct((m, LANES), F32)],
        compiler_params=_cparams("parallel"),
        name="moe_route",
    )(x, nw, mod, rw)


def _row_copy(src_hbm, dst_ref, src_row, dst_row, sem):
    s0 = pl.multiple_of(src_row * ROW_CHUNKS, ROW_CHUNKS)
    d0 = pl.multiple_of(dst_row * ROW_PITCH, 8)
    return pltpu.make_async_copy(src_hbm.at[pl.ds(s0, ROW_CHUNKS), :], dst_ref.at[pl.ds(d0, ROW_CHUNKS), :], sem)


def _gather_kernel(nv_ref, idx_ref, src_hbm, o_ref, buf_ref, sem, *, rows):
    blk = pl.program_id(0)
    n_valid = nv_ref[blk]

    @pl.when(blk == 0)
    def _():
        buf_ref[...] = jnp.zeros_like(buf_ref)

    def start(r, carry):
        _row_copy(src_hbm, buf_ref, idx_ref[0, r], r, sem).start()
        return carry

    def wait(r, carry):
        _row_copy(src_hbm, buf_ref, idx_ref[0, r], r, sem).wait()
        return carry

    lax.fori_loop(0, n_valid, start, 0)
    lax.fori_loop(0, n_valid, wait, 0)
    keep = lax.broadcasted_iota(jnp.int32, (rows, LANES), 0) < n_valid
    for cb in range(ROW_CHUNKS):
        val = buf_ref[pl.ds(cb, rows, stride=ROW_PITCH), :]
        o_ref[:, cb * LANES:(cb + 1) * LANES] = jnp.where(keep, val, 0.0).astype(o_ref.dtype)


def _gather_call(h_rows, src_idx, n_valid, rows):
    p = src_idx.shape[0]
    d = ROW_CHUNKS * LANES
    return pl.pallas_call(
        functools.partial(_gather_kernel, rows=rows),
        grid_spec=pltpu.PrefetchScalarGridSpec(
            num_scalar_prefetch=1,
            grid=(p // rows,),
            in_specs=[pl.BlockSpec((None, 1, rows), lambda i, n: (i, 0, 0), memory_space=pltpu.SMEM),
                      pl.BlockSpec(memory_space=pl.ANY)],
            out_specs=pl.BlockSpec((rows, d), lambda i, n: (i, 0)),
            scratch_shapes=[pltpu.VMEM((rows * ROW_PITCH, LANES), F32), pltpu.SemaphoreType.DMA(())]),
        out_shape=jax.ShapeDtypeStruct((p, d), BF16),
        compiler_params=_cparams("arbitrary"),
        name="moe_gather",
    )(n_valid, src_idx.reshape(p // rows, 1, rows), h_rows)


def _experts_kernel(te_ref, nt_ref, x_ref, wg_ref, wu_ref, wd_ref, o_ref, acc_ref):
    i, j = pl.program_id(0), pl.program_id(1)
    tm = x_ref.shape[0]
    last_j = pl.num_programs(1) - 1

    @pl.when(i < nt_ref[0])
    def _():
        x = x_ref[...]
        a = (_silu(_dot(x, wg_ref[...].astype(BF16))) * _dot(x, wu_ref[...].astype(BF16))).astype(BF16)
        part = _dot(a, wd_ref[...].astype(BF16))

        @pl.when(j == 0)
        def _():
            acc_ref[...] = part

        @pl.when(j != 0)
        def _():
            acc_ref[...] += part

        @pl.when(j == last_j)
        def _():
            for cb in range(ROW_CHUNKS):
                o_ref[pl.ds(cb, tm, stride=ROW_CHUNKS), :] = acc_ref[:, cb * LANES:(cb + 1) * LANES]

    @pl.when((i >= nt_ref[0]) & (j == last_j))
    def _():
        o_ref[...] = jnp.zeros_like(o_ref)


def _experts_call(tile_e, ntiles, xs, wg, wu, wd, layer, tm, tf):
    d, f = wg.shape[2:]
    p = xs.shape[0]
    nf = f // tf

    def row_blk(i, j, te, nt):
        return (jnp.minimum(i, nt[0] - 1), 0)

    def jj(i, j, nt):
        return jnp.where(i < nt[0], j, nf - 1)

    def w_in_blk(i, j, te, nt):
        return (layer, te[i], 0, jj(i, j, nt))

    def w_out_blk(i, j, te, nt):
        return (layer, te[i], jj(i, j, nt), 0)

    return pl.pallas_call(
        _experts_kernel,
        grid_spec=pltpu.PrefetchScalarGridSpec(
            num_scalar_prefetch=2,
            grid=(p // tm, nf),
            in_specs=[pl.BlockSpec((tm, d), row_blk),
                      pl.BlockSpec((None, None, d, tf), w_in_blk),
                      pl.BlockSpec((None, None, d, tf), w_in_blk),
                      pl.BlockSpec((None, None, tf, d), w_out_blk)],
            out_specs=pl.BlockSpec((tm * ROW_CHUNKS, LANES), lambda i, j, te, nt: (i, 0)),
            scratch_shapes=[pltpu.VMEM((tm, d), F32)]),
        out_shape=jax.ShapeDtypeStruct((p * ROW_CHUNKS, LANES), F32),
        compiler_params=_cparams("arbitrary", "arbitrary"),
        name="moe_experts",
    )(tile_e, ntiles, xs, wg, wu, wd)


def _combine_kernel(idx_ref, x_ref, r_ref, mod_ref, ys_hbm, o_ref, buf_ref, sem):
    tc = x_ref.shape[0]
    n_rows = TOP_K * tc

    def start(r, carry):
        _row_copy(ys_hbm, buf_ref, idx_ref[0, r], r, sem).start()
        return carry

    def wait(r, carry):
        _row_copy(ys_hbm, buf_ref, idx_ref[0, r], r, sem).wait()
        return carry

    lax.fori_loop(0, n_rows, start, 0, unroll=8)
    lax.fori_loop(0, n_rows, wait, 0, unroll=8)
    w1 = r_ref[:, 2:3]
    w2 = r_ref[:, 3:4]
    for cb in range(ROW_CHUNKS):
        cs = slice(cb * LANES, (cb + 1) * LANES)
        y1 = buf_ref[pl.ds(cb, tc, stride=ROW_PITCH), :]
        y2 = buf_ref[pl.ds(tc * ROW_PITCH + cb, tc, stride=ROW_PITCH), :]
        o_ref[:, cs] = x_ref[:, cs] + mod_ref[5:6, cs] * (y1 * w1 + y2 * w2)


def _combine_call(dest, x, route, mod, ys, mc, sd, tc):
    m, d = x.shape
    midx = functools.partial(_mod_row_index, tm=tc, mc=mc, sd=sd)
    return pl.pallas_call(
        _combine_kernel,
        grid=(m // tc,),
        in_specs=[pl.BlockSpec((None, 1, TOP_K * tc), lambda i: (i, 0, 0), memory_space=pltpu.SMEM),
                  pl.BlockSpec((tc, d), lambda i: (i, 0)),
                  pl.BlockSpec((tc, LANES), lambda i: (i, 0)),
                  pl.BlockSpec((None, 6, d), lambda i: (midx(i), 0, 0)),
                  pl.BlockSpec(memory_space=pl.ANY)],
        out_specs=pl.BlockSpec((tc, d), lambda i: (i, 0)),
        out_shape=jax.ShapeDtypeStruct((m, d), F32),
        scratch_shapes=[pltpu.VMEM((TOP_K * tc * ROW_PITCH, LANES), F32), pltpu.SemaphoreType.DMA(())],
        compiler_params=_cparams("arbitrary"),
        name="moe_combine",
    )(dest, x, route, mod, ys)


def _moe_plan(route, n_experts, tm, n_tiles, tc, gather_rows):
    m = route.shape[0]
    e_flat = jnp.concatenate([route[:, 0], route[:, 1]]).astype(jnp.int32)
    onehot = (e_flat[:, None] == jnp.arange(n_experts, dtype=jnp.int32)[None, :]).astype(jnp.int32)
    csum = jnp.cumsum(onehot, axis=0)
    cnt = csum[-1]
    rank = jnp.take_along_axis(csum, e_flat[:, None], axis=1)[:, 0] - 1
    gsz = ((cnt + tm - 1) // tm) * tm
    off_end = jnp.cumsum(gsz)
    off = off_end - gsz
    dest = off[e_flat] + rank
    n_used = off_end[-1] // tm

    def owner(starts):
        return jnp.minimum(jnp.sum((off_end[None, :] <= starts[:, None]).astype(jnp.int32), axis=1), n_experts - 1)

    tiles = jnp.arange(n_tiles, dtype=jnp.int32)
    tile_e = owner(tiles * tm)
    tile_e = jnp.where(tiles < n_used, tile_e, tile_e[jnp.maximum(n_used - 1, 0)])
    blocks = jnp.arange(n_tiles * tm // gather_rows, dtype=jnp.int32) * gather_rows
    n_valid = jnp.where(blocks < off_end[-1], jnp.clip((off + cnt)[owner(blocks)] - blocks, 0, gather_rows), 0)
    tok = jnp.tile(jnp.arange(m, dtype=jnp.int32), TOP_K)
    src = jnp.zeros((n_tiles * tm,), jnp.int32).at[dest].set(tok)
    dest_tiles = jnp.concatenate([dest[:m].reshape(m // tc, 1, tc), dest[m:].reshape(m // tc, 1, tc)], axis=2)
    i32 = lambda a: a.astype(jnp.int32)
    return src, dest_tiles, i32(tile_e), i32(n_used.reshape(1)), i32(n_valid)


def _moe_layer(x, nw, mod, rw, wg, wu, wd, layer, mc, sd, tm_route, tm, tf, tc, gather_rows):
    m, d = x.shape
    n_experts = wg.shape[1]
    assert tm % gather_rows == 0
    n_tiles = -(-(TOP_K * m) // tm) + n_experts
    h_rows, route = _route_call(x, nw, mod, rw, mc, sd, tm_route, n_experts)
    src, dest_tiles, tile_e, n_used, n_valid = _moe_plan(route, n_experts, tm, n_tiles, tc, gather_rows)
    xs = _gather_call(h_rows, src, n_valid, gather_rows)
    ys = _experts_call(tile_e, n_used, xs, wg, wu, wd, layer, tm, tf)
    return _combine_call(dest_tiles, x, route, mod, ys, mc, sd, tc)


def _final_kernel(x_ref, w_ref, o_ref):
    o_ref[...] = _rms(x_ref[...], w_ref[...])


def _final_call(x, w, row0, rows, tm):
    d = x.shape[1]
    blk0 = row0 // tm
    return pl.pallas_call(
        _final_kernel,
        grid=(rows // tm,),
        in_specs=[pl.BlockSpec((tm, d), lambda i: (blk0 + i, 0)), pl.BlockSpec((1, d), lambda i: (0, 0))],
        out_specs=pl.BlockSpec((tm, d), lambda i: (i, 0)),
        out_shape=jax.ShapeDtypeStruct((rows, d), F32),
        compiler_params=_cparams("parallel"),
        name="final_norm",
    )(x, w)


def _sincos_2d(rows, cols, dim):
    quarter = dim // 4
    omega = 1.0 / (10000.0 ** (jnp.arange(quarter, dtype=F32) / quarter))
    r = jnp.arange(rows, dtype=F32)[:, None] * omega
    cc = jnp.arange(cols, dtype=F32)[:, None] * omega
    r_emb = jnp.concatenate([jnp.sin(r), jnp.cos(r)], axis=-1)
    c_emb = jnp.concatenate([jnp.sin(cc), jnp.cos(cc)], axis=-1)
    emb = jnp.concatenate([jnp.broadcast_to(r_emb[:, None, :], (rows, cols, dim // 2)),
                           jnp.broadcast_to(c_emb[None, :, :], (rows, cols, dim // 2))], axis=-1)
    return emb.reshape(rows * cols, dim)


def _pad_lanes(a):
    return jnp.pad(a, ((0, 0), (0, LANES - a.shape[1])))


def kernel(x_prompt, x_sample, c, state_delta, c_ctx, norm1_w, norm2_w, w_mod, b_mod, w_in, w_out, qkv_conv_w, delta_a_log, delta_dt_bias, delta_norm_w, sgu_norm_w, sgu_w, sgu_b, sconv_w, pool_w, pool_scale, ffn_w_gate, ffn_w_up, ffn_w_down, router_w, moe_w_gate, moe_w_up, moe_w_down, final_norm_w):
    bc, sc, d = x_prompt.shape
    bd, sd, _ = x_sample.shape
    depth = w_in.shape[0]
    mc, ml = bc * sc, bd * sd
    m = mc + ml
    w_a = H_A * LANES
    w_b, w_c, w_d = sgu_norm_w.shape[1], sconv_w.shape[1], pool_scale.shape[1]
    assert d == ROW_CHUNKS * LANES and mc % sd == 0 and w_a == w_b == w_c == w_d == d // 4
    assert delta_a_log.shape[1:] == (2, H_A) and delta_norm_w.shape[1] == LANES

    tm_mm = 512 if m % 512 == 0 else 256
    grid_w = 64
    pos = _sincos_2d(sd // grid_w, grid_w, d)
    x = jnp.concatenate([x_prompt.reshape(mc, d), (x_sample + pos[None]).reshape(ml, d)], axis=0)

    rm = -(-(1 + bd) // 8) * 8
    cond = jnp.zeros((rm, d), F32).at[0].set(c_ctx).at[1:1 + bd].set(c)
    mod_all = _mod_call(cond, w_mod, b_mod).reshape(depth, rm, 6, d)

    o0 = 3 * w_a
    o1 = o0 + w_a
    o3 = o1 + 4 * H_A
    o4 = o3 + 2 * w_b
    o5 = o4 + 3 * w_c
    blk0 = mc // sd
    w_main_all = jnp.concatenate([w_in[:, :, :o1], w_in[:, :, o3:]], axis=2).astype(BF16)
    w_ab_all = jnp.pad(w_in[:, :, o1:o3], ((0, 0), (0, 0), (0, LANES - (o3 - o1)))).astype(BF16)
    w_out_b = w_out.astype(BF16)
    ffn_b = [w.astype(BF16) for w in (ffn_w_gate, ffn_w_up, ffn_w_down)]
    moe_w = (moe_w_gate, moe_w_up, moe_w_down.astype(BF16))
    ctx_states = []
    for l in range(depth):
        mod = mod_all[l]
        tm_in = 1024 if (mc % 1024 == 0 and sd % 1024 == 0) else tm_mm
        proj, ab = _inproj_call(x, norm1_w[l][None], mod, w_main_all, w_ab_all, l, mc, sd, tm_in, 1024)

        conv_t = qkv_conv_w[l].T
        alog = _pad_lanes(delta_a_log[l].reshape(1, 2 * H_A))
        dtb = _pad_lanes(delta_dt_bias[l].reshape(1, 2 * H_A))
        nwa = delta_norm_w[l][None]
        a_ctx, s_ctx = _delta_call(proj, ab, conv_t, alog, dtb, nwa, None, l,
                                   t_len=sc, nseq=bc, blk0=0, write_state=True)
        (a_lat,) = _delta_call(proj, ab, conv_t, alog, dtb, nwa, state_delta, l,
                               t_len=sd, nseq=bd, blk0=blk0, write_state=False)
        ctx_states.append(s_ctx)

        def both_paths(kern, name, col_block, width_in, params, pspecs, width_out):
            return (_seq_call(kern, name, proj, col_block, width_in, params, pspecs,
                              t_len=sc, nseq=bc, blk0=0, width_out=width_out),
                    _seq_call(kern, name, proj, col_block, width_in, params, pspecs,
                              t_len=sd, nseq=bd, blk0=blk0, width_out=width_out))

        n_g = sgu_w.shape[1]
        b_ctx, b_lat = both_paths(
            _sgu_kernel, "sgu", o1 // (2 * w_b), 2 * w_b,
            [sgu_norm_w[l][None], sgu_w[l].astype(BF16), sgu_b[l].reshape(n_g, SGU_CHUNK, 1)],
            [pl.BlockSpec((1, w_b), lambda b: (0, 0)),
             pl.BlockSpec((n_g, SGU_CHUNK, SGU_CHUNK), lambda b: (0, 0, 0)),
             pl.BlockSpec((n_g, SGU_CHUNK, 1), lambda b: (0, 0, 0))], w_b)
        c_ctx_mix, c_lat = both_paths(
            _sconv_kernel, "sconv", (o1 + 2 * w_b) // (3 * w_c), 3 * w_c,
            [sconv_w[l].T], [pl.BlockSpec((3, w_c), lambda b: (0, 0))], w_c)
        n_gd = pool_w.shape[1]
        d_ctx, d_lat = both_paths(
            _pool_kernel, "pool", (o1 + 2 * w_b + 3 * w_c) // w_d, w_d,
            [pool_w[l].astype(BF16), pool_scale[l][None]],
            [pl.BlockSpec((n_gd, LANES, LANES), lambda b: (0, 0, 0)),
             pl.BlockSpec((1, w_d), lambda b: (0, 0))], w_d)

        x = _outproj_call(x, (a_ctx, b_ctx, c_ctx_mix, d_ctx), (a_lat, b_lat, c_lat, d_lat),
                          w_out_b, l, mod, mc, sd, tm_mm)

        jl = l // 2
        if l % 2 == 0:
            x = _ffn_call(x, norm2_w[l][None], mod, *ffn_b, jl, mc, sd, tm_mm, 512)
        else:
            x = _moe_layer(x, norm2_w[l][None], mod, _pad_lanes(router_w[jl]), *moe_w, jl,
                           mc, sd, tm_route=256, tm=640, tf=512, tc=256, gather_rows=320)

    y_prompt = _final_call(x, final_norm_w[None], 0, mc, tm_mm).reshape(bc, sc, d)
    y_sample = _final_call(x, final_norm_w[None], mc, ml, tm_mm).reshape(bd, sd, d)
    new_state = jnp.stack(ctx_states, axis=1).astype(x_prompt.dtype)
    return (y_prompt, y_sample, new_state)
```

```python
import functools

import jax
import jax.numpy as jnp
from jax import lax
from jax.experimental import pallas as pl
from jax.experimental.pallas import tpu as pltpu

F32 = jnp.float32
BF16 = jnp.bfloat16

LANES = 128
ROW_CHUNKS = 16
ROW_PITCH = 24
DELTA_CHUNK = 64
SGU_CHUNK = 128
H_A = 4
POOL_WINDOWS = (2, 4, 8, 16)
TOP_K = 2
VMEM_LIMIT = 56 * 1024 * 1024


def _cparams(*sem):
    return pltpu.CompilerParams(dimension_semantics=sem, vmem_limit_bytes=VMEM_LIMIT)


def _silu(x):
    return x / (1.0 + jnp.exp(-x))


def _sigmoid(x):
    return 1.0 / (1.0 + jnp.exp(-x))


def _softplus(x):
    return jnp.maximum(x, 0.0) + jnp.log1p(jnp.exp(-jnp.abs(x)))


def _gelu_tanh(x):
    return 0.5 * x * (1.0 + jnp.tanh(0.7978845608028654 * (x + 0.044715 * (x * x * x))))


def _rms(x, w, eps=1e-6):
    return x * lax.rsqrt(jnp.mean(x * x, axis=-1, keepdims=True) + eps) * w


def _dot(a, b):
    return jnp.dot(a, b, preferred_element_type=F32)


def _split(a):
    hi = a.astype(BF16)
    lo = (a - hi.astype(F32)).astype(BF16)
    return hi, lo


def _dot3(a, b):
    a_hi, a_lo = _split(a)
    b_hi, b_lo = _split(b)
    return _dot(a_hi, b_hi) + _dot(a_hi, b_lo) + _dot(a_lo, b_hi)


def _bmm(a, b):
    return jnp.einsum('nij,njk->nik', a, b, preferred_element_type=F32)


def _bmm16(a, b):
    return _bmm(a.astype(BF16), b.astype(BF16))


def _mod_kernel(c_ref, w_ref, b_ref, o_ref):
    a = _silu(c_ref[...]).astype(BF16)
    o_ref[...] = _dot(a, w_ref[...].astype(BF16)) + b_ref[...]


def _mod_call(cond, w_mod, b_mod, tn=1024):
    depth, d, n = w_mod.shape
    rm = cond.shape[0]
    return pl.pallas_call(
        _mod_kernel,
        grid=(depth, n // tn),
        in_specs=[pl.BlockSpec((rm, d), lambda l, j: (0, 0)),
                  pl.BlockSpec((None, d, tn), lambda l, j: (l, 0, j)),
                  pl.BlockSpec((None, 1, tn), lambda l, j: (l, 0, j))],
        out_specs=pl.BlockSpec((None, rm, tn), lambda l, j: (l, 0, j)),
        out_shape=jax.ShapeDtypeStruct((depth, rm, n), F32),
        compiler_params=_cparams("parallel", "parallel"),
        name="adaln_mod",
    )(cond, w_mod, b_mod.reshape(depth, 1, n))


def _mod_row_index(i, tm, mc, sd):
    r0 = i * tm
    return jnp.where(r0 < mc, 0, 1 + (r0 - mc) // sd)


def _inproj_kernel(x_ref, nw_ref, mod_ref, w_ref, wab_ref, proj_ref, ab_ref, h_ref):
    @pl.when(pl.program_id(1) == 0)
    def _():
        h = _rms(x_ref[...], nw_ref[...]) * (1.0 + mod_ref[1:2, :]) + mod_ref[0:1, :]
        hb = h.astype(BF16)
        h_ref[...] = hb
        ab_ref[...] = _dot(hb, wab_ref[...])

    proj_ref[...] = _dot(h_ref[...], w_ref[...])


def _inproj_call(x, nw, mod, w_main, w_ab, layer, mc, sd, tm, tn):
    m, d = x.shape
    n = w_main.shape[2]
    midx = functools.partial(_mod_row_index, tm=tm, mc=mc, sd=sd)
    return pl.pallas_call(
        _inproj_kernel,
        grid=(m // tm, n // tn),
        in_specs=[pl.BlockSpec((tm, d), lambda i, j: (i, 0)),
                  pl.BlockSpec((1, d), lambda i, j: (0, 0)),
                  pl.BlockSpec((None, 6, d), lambda i, j: (midx(i), 0, 0)),
                  pl.BlockSpec((None, d, tn), lambda i, j: (layer, 0, j)),
                  pl.BlockSpec((None, d, LANES), lambda i, j: (layer, 0, 0))],
        out_specs=[pl.BlockSpec((tm, tn), lambda i, j: (i, j)),
                   pl.BlockSpec((tm, LANES), lambda i, j: (i, 0))],
        out_shape=[jax.ShapeDtypeStruct((m, n), F32), jax.ShapeDtypeStruct((m, LANES), F32)],
        scratch_shapes=[pltpu.VMEM((tm, d), BF16)],
        compiler_params=_cparams("parallel", "arbitrary"),
        name="in_proj",
    )(x, nw, mod, w_main, w_ab)


def _shift_rows(x, d, row, t_len):
    if d == 0:
        return x
    y = pltpu.roll(x, (-d) % t_len, axis=0)
    ok = (row + d >= 0) & (row + d < t_len)
    return jnp.where(ok, y, 0.0)


def _conv3(x, w, row, t_len):
    return (_shift_rows(x, -1, row, t_len) * w[0:1, :] + x * w[1:2, :]
            + _shift_rows(x, 1, row, t_len) * w[2:3, :])


def _chunk_cumsum(g, row, t_len, reverse):
    pos = row % DELTA_CHUNK
    s = 1
    while s < DELTA_CHUNK:
        if reverse:
            g = g + jnp.where(pos < DELTA_CHUNK - s, pltpu.roll(g, t_len - s, axis=0), 0.0)
        else:
            g = g + jnp.where(pos >= s, pltpu.roll(g, s, axis=0), 0.0)
        s *= 2
    return g


def _unit_tri_inverse(lmat, r, c):
    eye = jnp.where(r == c, 1.0, 0.0)

    def same_block(b):
        return (r // b) == (c // b)

    x = jnp.where(same_block(8), -lmat, 0.0)
    x2 = _bmm16(x, x)
    x4 = _bmm16(x2, x2)
    p = _bmm16(_bmm16(eye + x, eye + x2), eye + x4)
    b = 8
    while b < DELTA_CHUNK:
        off = jnp.where(same_block(2 * b) & jnp.logical_not(same_block(b)), lmat, 0.0)
        pb = p.astype(BF16)
        p = p - _bmm16(_bmm16(pb, off), pb)
        b *= 2
    return p


def _delta_kernel(*refs, t_len, hps, has_s0, write_state):
    (q_ref, k_ref, v_ref, ga_ref, ab_ref, cq_ref, ck_ref, cv_ref, alog_ref, dtb_ref, nw_ref), rest = refs[:11], refs[11:]
    if has_s0:
        s0_ref, rest = rest[0], rest[1:]
    o_ref, rest = rest[0], rest[1:]
    if write_state:
        sout_ref = rest[0]

    n_ch = t_len // DELTA_CHUNK
    cl = DELTA_CHUNK
    row = lax.broadcasted_iota(jnp.int32, (t_len, LANES), 0)
    lane = lax.broadcasted_iota(jnp.int32, (t_len, LANES), 1)
    r64 = lax.broadcasted_iota(jnp.int32, (cl, cl), 0)
    c64 = lax.broadcasted_iota(jnp.int32, (cl, cl), 1)

    def l2n(x):
        return x * lax.rsqrt(jnp.sum(x * x, axis=-1, keepdims=True) + 1e-6)

    def column(a, idx):
        col = jnp.sum(jnp.where(lane == idx, a, 0.0), axis=1, keepdims=True)
        return jnp.broadcast_to(col, (t_len, LANES))

    def chunks(a):
        return a.reshape(n_ch, cl, LANES)

    ab = ab_ref[...]
    g_all = -jnp.exp(alog_ref[...]) * _softplus(ab + dtb_ref[...])
    b_all = _sigmoid(ab)

    chains = []
    for hh in range(hps):
        head = pl.program_id(1) * hps + hh
        cs = slice(hh * LANES, (hh + 1) * LANES)
        q = l2n(_silu(_conv3(q_ref[:, cs], cq_ref[:, cs], row, t_len))) * (LANES ** -0.5)
        k = l2n(_silu(_conv3(k_ref[:, cs], ck_ref[:, cs], row, t_len)))
        v = _silu(_conv3(v_ref[:, cs], cv_ref[:, cs], row, t_len))
        k3 = chunks(k)
        kb16 = k3.astype(BF16)
        kk = jnp.einsum('ncd,nsd->ncs', kb16, kb16, preferred_element_type=F32)
        qk = jnp.einsum('ncd,nsd->ncs', chunks(q).astype(BF16), kb16, preferred_element_type=F32)
        for direction in range(2):
            rev = direction == 1
            g = column(g_all, direction * H_A + head)
            beta = column(b_all, 2 * H_A + direction * H_A + head)
            gc = _chunk_cumsum(g, row, t_len, rev)
            gc3 = chunks(gc)
            gc_rows = jnp.swapaxes(gc3, 1, 2)[:, :cl, :]
            dmat = gc3[:, :, :cl] - gc_rows
            if rev:
                incl, strict = c64 >= r64, c64 > r64
            else:
                incl, strict = c64 <= r64, c64 < r64
            decay = jnp.exp(jnp.where(incl, dmat, 0.0))
            lmat = jnp.where(strict, chunks(beta)[:, :, :cl] * kk * decay, 0.0)
            tinv = _unit_tri_inverse(lmat, r64, c64).astype(BF16)
            egc = jnp.exp(gc)
            last = 0 if rev else cl - 1
            glast = gc3[:, last:last + 1, :]
            if has_s0:
                s_init = s0_ref[direction, hh]
            else:
                s_init = jnp.zeros((LANES, LANES), F32)
            uw = _bmm16(tinv, jnp.concatenate([chunks(v * beta), chunks(k * beta * egc)], axis=2))
            attn = jnp.where(incl, qk * decay, 0.0)
            kdt = jnp.swapaxes(k3 * jnp.exp(glast - gc3), 1, 2)
            chains.append(dict(
                hh=hh, direction=direction, s=s_init, outs=[None] * n_ch,
                order=list(range(n_ch - 1, -1, -1) if rev else range(n_ch)),
                u=uw[:, :, :LANES],
                w_qg=jnp.concatenate([uw[:, :, LANES:], chunks(q * egc)], axis=1).astype(BF16),
                attn_kdt=jnp.concatenate([attn, kdt], axis=1).astype(BF16),
                gl=jnp.exp(glast)))

    for t in range(n_ch):
        for ch in chains:
            n = ch['order'][t]
            ws_qs = _dot(ch['w_qg'][n], ch['s'].astype(BF16))
            v_new = ch['u'][n] - ws_qs[:cl]
            av_kv = _dot(ch['attn_kdt'][n], v_new.astype(BF16))
            ch['outs'][n] = ws_qs[cl:] + av_kv[:cl]
            ch['s'] = ch['s'] * ch['gl'][n] + av_kv[cl:]

    for ch in chains:
        if write_state:
            sout_ref[ch['direction'], ch['hh']] = ch['s']
    for hh in range(hps):
        cs = slice(hh * LANES, (hh + 1) * LANES)
        fwd, bwd = chains[2 * hh], chains[2 * hh + 1]
        o = jnp.concatenate([a + b for a, b in zip(fwd['outs'], bwd['outs'])], axis=0)
        o_ref[:, cs] = (_rms(o, nw_ref[...]) * _silu(ga_ref[:, cs])).astype(o_ref.dtype)


def _delta_call(proj, ab, conv_t, alog, dtb, nw, s0, layer, *, t_len, nseq, blk0, write_state, hps=2):
    has_s0 = s0 is not None
    w_a = H_A * LANES
    wh = hps * LANES
    nhb = H_A // hps

    def pspec(part):
        return pl.BlockSpec((t_len, wh), lambda b, h: (blk0 + b, part * nhb + h))

    def cspec(part):
        return pl.BlockSpec((3, wh), lambda b, h: (0, part * nhb + h))

    row1 = pl.BlockSpec((1, LANES), lambda b, h: (0, 0))
    in_specs = [pspec(0), pspec(1), pspec(2), pspec(3),
                pl.BlockSpec((t_len, LANES), lambda b, h: (blk0 + b, 0)),
                cspec(0), cspec(1), cspec(2), row1, row1, row1]
    args = [proj, proj, proj, proj, ab, conv_t, conv_t, conv_t, alog, dtb, nw]
    if has_s0:
        in_specs.append(pl.BlockSpec((None, None, 2, hps, LANES, LANES), lambda b, h: (b, layer, 0, h, 0, 0)))
        args.append(s0)
    out_specs = [pl.BlockSpec((t_len, wh), lambda b, h: (b, h))]
    out_shape = [jax.ShapeDtypeStruct((nseq * t_len, w_a), BF16)]
    if write_state:
        out_specs.append(pl.BlockSpec((None, 2, hps, LANES, LANES), lambda b, h: (b, 0, h, 0, 0)))
        out_shape.append(jax.ShapeDtypeStruct((nseq, 2, H_A, LANES, LANES), F32))

    return pl.pallas_call(
        functools.partial(_delta_kernel, t_len=t_len, hps=hps, has_s0=has_s0, write_state=write_state),
        grid=(nseq, nhb),
        in_specs=in_specs,
        out_specs=out_specs,
        out_shape=out_shape,
        compiler_params=_cparams("parallel", "parallel"),
        name=f"delta_T{t_len}",
    )(*args)


def _sgu_kernel(z_ref, nw_ref, w_ref, b_ref, o_ref, *, t_len):
    wb = o_ref.shape[1]
    z = _gelu_tanh(z_ref[...])
    u, v = z[:, :wb], z[:, wb:]
    mu = jnp.mean(v, axis=-1, keepdims=True)
    vc = v - mu
    vn = (vc * lax.rsqrt(jnp.mean(vc * vc, axis=-1, keepdims=True) + 1e-5) * nw_ref[...]).astype(BF16)
    for n in range(t_len // SGU_CHUNK):
        rs = slice(n * SGU_CHUNK, (n + 1) * SGU_CHUNK)
        for g in range(wb // LANES):
            cs = slice(g * LANES, (g + 1) * LANES)
            sp = _dot(w_ref[g], vn[rs, cs]) + b_ref[g]
            o_ref[rs, cs] = (u[rs, cs] * sp).astype(o_ref.dtype)


def _sconv_kernel(c_ref, w_ref, o_ref, *, t_len):
    wc = o_ref.shape[1]
    row = lax.broadcasted_iota(jnp.int32, (t_len, wc), 0)
    x = c_ref[...]
    y = _conv3(x[:, wc:2 * wc] * x[:, 2 * wc:], w_ref[...], row, t_len)
    o_ref[...] = (x[:, :wc] * y).astype(o_ref.dtype)


def _pool_kernel(p_ref, w_ref, sc_ref, o_ref, *, t_len):
    row = lax.broadcasted_iota(jnp.int32, (t_len, LANES), 0)
    for j, win in enumerate(POOL_WINDOWS):
        cs = slice(j * LANES, (j + 1) * LANES)
        x = p_ref[:, cs]
        half = win // 2
        acc = x
        for d in range(-half, win - half):
            if d != 0:
                acc = acc + _shift_rows(x, d, row, t_len)
        cnt = (jnp.minimum(row + (win - half), t_len) - jnp.maximum(row - half, 0)).astype(F32)
        pooled = acc / cnt - x
        o_ref[:, cs] = (_dot(pooled.astype(BF16), w_ref[j]) * sc_ref[:, cs]).astype(o_ref.dtype)


def _seq_call(kern, name, proj, col_block, width_in, params, pspecs, *, t_len, nseq, blk0, width_out):
    return pl.pallas_call(
        functools.partial(kern, t_len=t_len),
        grid=(nseq,),
        in_specs=[pl.BlockSpec((t_len, width_in), lambda b: (blk0 + b, col_block))] + pspecs,
        out_specs=pl.BlockSpec((t_len, width_out), lambda b: (b, 0)),
        out_shape=jax.ShapeDtypeStruct((nseq * t_len, width_out), BF16),
        compiler_params=_cparams("parallel"),
        name=f"{name}_T{t_len}",
    )(proj, *params)


def _outproj_kernel(x_ref, *refs, n_ctx_tiles):
    n_mix = (len(refs) - 3) // 2
    ctx_refs, lat_refs = refs[:n_mix], refs[n_mix:2 * n_mix]
    w_ref, mod_ref, o_ref = refs[2 * n_mix:]

    def run(mix_refs):
        acc = None
        for i, m_ref in enumerate(mix_refs):
            wq = m_ref.shape[1]
            part = _dot(m_ref[...], w_ref[i * wq:(i + 1) * wq, :])
            acc = part if acc is None else acc + part
        o_ref[...] = x_ref[...] + mod_ref[2:3, :] * acc

    is_ctx = pl.program_id(0) < n_ctx_tiles
    pl.when(is_ctx)(lambda: run(ctx_refs))
    pl.when(jnp.logical_not(is_ctx))(lambda: run(lat_refs))


def _outproj_call(x, mixes_ctx, mixes_lat, w_out, layer, mod, mc, sd, tm):
    m, d = x.shape
    wq = mixes_ctx[0].shape[1]
    nct = mc // tm
    midx = functools.partial(_mod_row_index, tm=tm, mc=mc, sd=sd)
    cspec = pl.BlockSpec((tm, wq), lambda i: (jnp.minimum(i, nct - 1), 0))
    lspec = pl.BlockSpec((tm, wq), lambda i: (jnp.maximum(i - nct, 0), 0))
    n_mix = len(mixes_ctx)
    return pl.pallas_call(
        functools.partial(_outproj_kernel, n_ctx_tiles=nct),
        grid=(m // tm,),
        in_specs=[pl.BlockSpec((tm, d), lambda i: (i, 0))] + [cspec] * n_mix + [lspec] * n_mix
                 + [pl.BlockSpec((None, d, d), lambda i: (layer, 0, 0)),
                    pl.BlockSpec((None, 6, d), lambda i: (midx(i), 0, 0))],
        out_specs=pl.BlockSpec((tm, d), lambda i: (i, 0)),
        out_shape=jax.ShapeDtypeStruct((m, d), F32),
        compiler_params=_cparams("parallel"),
        name="out_proj",
    )(x, *mixes_ctx, *mixes_lat, w_out, mod)


def _ffn_kernel(x_ref, nw_ref, mod_ref, wg_ref, wu_ref, wd_ref, o_ref, h_ref, acc_ref):
    j = pl.program_id(1)

    @pl.when(j == 0)
    def _():
        h = _rms(x_ref[...], nw_ref[...]) * (1.0 + mod_ref[4:5, :]) + mod_ref[3:4, :]
        h_ref[...] = h.astype(BF16)
        acc_ref[...] = jnp.zeros_like(acc_ref)

    h = h_ref[...]
    a = (_silu(_dot(h, wg_ref[...])) * _dot(h, wu_ref[...])).astype(BF16)
    acc_ref[...] += _dot(a, wd_ref[...])

    @pl.when(j == pl.num_programs(1) - 1)
    def _():
        o_ref[...] = x_ref[...] + mod_ref[5:6, :] * acc_ref[...]


def _ffn_call(x, nw, mod, wg, wu, wd, layer, mc, sd, tm, tf):
    m, d = x.shape
    f = wg.shape[2]
    midx = functools.partial(_mod_row_index, tm=tm, mc=mc, sd=sd)
    return pl.pallas_call(
        _ffn_kernel,
        grid=(m // tm, f // tf),
        in_specs=[pl.BlockSpec((tm, d), lambda i, j: (i, 0)),
                  pl.BlockSpec((1, d), lambda i, j: (0, 0)),
                  pl.BlockSpec((None, 6, d), lambda i, j: (midx(i), 0, 0)),
                  pl.BlockSpec((None, d, tf), lambda i, j: (layer, 0, j)),
                  pl.BlockSpec((None, d, tf), lambda i, j: (layer, 0, j)),
                  pl.BlockSpec((None, tf, d), lambda i, j: (layer, j, 0))],
        out_specs=pl.BlockSpec((tm, d), lambda i, j: (i, 0)),
        out_shape=jax.ShapeDtypeStruct((m, d), F32),
        scratch_shapes=[pltpu.VMEM((tm, d), BF16), pltpu.VMEM((tm, d), F32)],
        compiler_params=_cparams("parallel", "arbitrary"),
        name="ffn_dense",
    )(x, nw, mod, wg, wu, wd)


def _route_kernel(x_ref, nw_ref, mod_ref, rw_ref, h_ref, r_ref, *, n_experts):
    tm = x_ref.shape[0]
    h = _rms(x_ref[...], nw_ref[...]) * (1.0 + mod_ref[4:5, :]) + mod_ref[3:4, :]
    for cb in range(ROW_CHUNKS):
        h_ref[pl.ds(cb, tm, stride=ROW_CHUNKS), :] = h[:, cb * LANES:(cb + 1) * LANES]
    lane = lax.broadcasted_iota(jnp.int32, (tm, LANES), 1).astype(F32)
    neg = jnp.float32(-jnp.inf)
    logits = jnp.where(lane < n_experts, _dot3(h, rw_ref[...]), neg)
    m1 = jnp.max(logits, axis=-1, keepdims=True)
    i1 = jnp.min(jnp.where(logits == m1, lane, float(LANES)), axis=-1, keepdims=True)
    rest = jnp.where(lane == i1, neg, logits)
    m2 = jnp.max(rest, axis=-1, keepdims=True)
    i2 = jnp.min(jnp.where(rest == m2, lane, float(LANES)), axis=-1, keepdims=True)
    e2 = jnp.exp(m2 - m1)
    den = 1.0 + e2
    r_ref[...] = jnp.where(lane == 0.0, i1,
                           jnp.where(lane == 1.0, i2,
                                     jnp.where(lane == 2.0, 1.0 / den, jnp.where(lane == 3.0, e2 / den, 0.0))))


def _route_call(x, nw, mod, rw, mc, sd, tm, n_experts):
    m, d = x.shape
    midx = functools.partial(_mod_row_index, tm=tm, mc=mc, sd=sd)
    return pl.pallas_call(
        functools.partial(_route_kernel, n_experts=n_experts),
        grid=(m // tm,),
        in_specs=[pl.BlockSpec((tm, d), lambda i: (i, 0)),
                  pl.BlockSpec((1, d), lambda i: (0, 0)),
                  pl.BlockSpec((None, 6, d), lambda i: (midx(i), 0, 0)),
                  pl.BlockSpec((d, LANES), lambda i: (0, 0))],
        out_specs=[pl.BlockSpec((tm * ROW_CHUNKS, LANES), lambda i: (i, 0)),
                   pl.BlockSpec((tm, LANES), lambda i: (i, 0))],
        out_shape=[jax.ShapeDtypeStruct((m * ROW_CHUNKS, LANES), F32),
                   jax.ShapeDtypeStruct((m, LANES), F32)],
        compiler_params=_cparams("parallel"),
        name="moe_route",
    )(x, nw, mod, rw)


def _row_copy(src_hbm, dst_ref, src_row, dst_row, sem):
    s0 = pl.multiple_of(src_row * ROW_CHUNKS, ROW_CHUNKS)
    d0 = pl.multiple_of(dst_row * ROW_PITCH, 8)
    return pltpu.make_async_copy(src_hbm.at[pl.ds(s0, ROW_CHUNKS), :], dst_ref.at[pl.ds(d0, ROW_CHUNKS), :], sem)


def _gather_kernel(nv_ref, idx_ref, src_hbm, o_ref, buf_ref, sem, *, rows):
    blk = pl.program_id(0)
    n_valid = nv_ref[blk]

    @pl.when(blk == 0)
    def _():
        buf_ref[...] = jnp.zeros_like(buf_ref)

    def start(r, carry):
        _row_copy(src_hbm, buf_ref, idx_ref[0, r], r, sem).start()
        return carry

    def wait(r, carry):
        _row_copy(src_hbm, buf_ref, idx_ref[0, r], r, sem).wait()
        return carry

    lax.fori_loop(0, n_valid, start, 0)
    lax.fori_loop(0, n_valid, wait, 0)
    keep = lax.broadcasted_iota(jnp.int32, (rows, LANES), 0) < n_valid
    for cb in range(ROW_CHUNKS):
        val = buf_ref[pl.ds(cb, rows, stride=ROW_PITCH), :]
        o_ref[:, cb * LANES:(cb + 1) * LANES] = jnp.where(keep, val, 0.0).astype(o_ref.dtype)


def _gather_call(h_rows, src_idx, n_valid, rows):
    p = src_idx.shape[0]
    d = ROW_CHUNKS * LANES
    return pl.pallas_call(
        functools.partial(_gather_kernel, rows=rows),
        grid_spec=pltpu.PrefetchScalarGridSpec(
            num_scalar_prefetch=1,
            grid=(p // rows,),
            in_specs=[pl.BlockSpec((None, 1, rows), lambda i, n: (i, 0, 0), memory_space=pltpu.SMEM),
                      pl.BlockSpec(memory_space=pl.ANY)],
            out_specs=pl.BlockSpec((rows, d), lambda i, n: (i, 0)),
            scratch_shapes=[pltpu.VMEM((rows * ROW_PITCH, LANES), F32), pltpu.SemaphoreType.DMA(())]),
        out_shape=jax.ShapeDtypeStruct((p, d), BF16),
        compiler_params=_cparams("arbitrary"),
        name="moe_gather",
    )(n_valid, src_idx.reshape(p // rows, 1, rows), h_rows)


def _experts_kernel(te_ref, nt_ref, x_ref, wg_ref, wu_ref, wd_ref, o_ref, acc_ref):
    i, j = pl.program_id(0), pl.program_id(1)
    tm = x_ref.shape[0]
    last_j = pl.num_programs(1) - 1

    @pl.when(i < nt_ref[0])
    def _():
        x = x_ref[...]
        a = (_silu(_dot(x, wg_ref[...].astype(BF16))) * _dot(x, wu_ref[...].astype(BF16))).astype(BF16)
        part = _dot(a, wd_ref[...].astype(BF16))

        @pl.when(j == 0)
        def _():
            acc_ref[...] = part

        @pl.when(j != 0)
        def _():
            acc_ref[...] += part

        @pl.when(j == last_j)
        def _():
            for cb in range(ROW_CHUNKS):
                o_ref[pl.ds(cb, tm, stride=ROW_CHUNKS), :] = acc_ref[:, cb * LANES:(cb + 1) * LANES]

    @pl.when((i >= nt_ref[0]) & (j == last_j))
    def _():
        o_ref[...] = jnp.zeros_like(o_ref)


def _experts_call(tile_e, ntiles, xs, wg, wu, wd, layer, tm, tf):
    d, f = wg.shape[2:]
    p = xs.shape[0]
    nf = f // tf

    def row_blk(i, j, te, nt):
        return (jnp.minimum(i, nt[0] - 1), 0)

    def jj(i, j, nt):
        return jnp.where(i < nt[0], j, nf - 1)

    def w_in_blk(i, j, te, nt):
        return (layer, te[i], 0, jj(i, j, nt))

    def w_out_blk(i, j, te, nt):
        return (layer, te[i], jj(i, j, nt), 0)

    return pl.pallas_call(
        _experts_kernel,
        grid_spec=pltpu.PrefetchScalarGridSpec(
            num_scalar_prefetch=2,
            grid=(p // tm, nf),
            in_specs=[pl.BlockSpec((tm, d), row_blk),
                      pl.BlockSpec((None, None, d, tf), w_in_blk),
                      pl.BlockSpec((None, None, d, tf), w_in_blk),
                      pl.BlockSpec((None, None, tf, d), w_out_blk)],
            out_specs=pl.BlockSpec((tm * ROW_CHUNKS, LANES), lambda i, j, te, nt: (i, 0)),
            scratch_shapes=[pltpu.VMEM((tm, d), F32)]),
        out_shape=jax.ShapeDtypeStruct((p * ROW_CHUNKS, LANES), F32),
        compiler_params=_cparams("arbitrary", "arbitrary"),
        name="moe_experts",
    )(tile_e, ntiles, xs, wg, wu, wd)


def _combine_kernel(idx_ref, x_ref, r_ref, mod_ref, ys_hbm, o_ref, buf_ref, sem):
    tc = x_ref.shape[0]
    n_rows = TOP_K * tc

    def start(r, carry):
        _row_copy(ys_hbm, buf_ref, idx_ref[0, r], r, sem).start()
        return carry

    def wait(r, carry):
        _row_copy(ys_hbm, buf_ref, idx_ref[0, r], r, sem).wait()
        return carry

    lax.fori_loop(0, n_rows, start, 0, unroll=8)
    lax.fori_loop(0, n_rows, wait, 0, unroll=8)
    w1 = r_ref[:, 2:3]
    w2 = r_ref[:, 3:4]
    for cb in range(ROW_CHUNKS):
        cs = slice(cb * LANES, (cb + 1) * LANES)
        y1 = buf_ref[pl.ds(cb, tc, stride=ROW_PITCH), :]
        y2 = buf_ref[pl.ds(tc * ROW_PITCH + cb, tc, stride=ROW_PITCH), :]
        o_ref[:, cs] = x_ref[:, cs] + mod_ref[5:6, cs] * (y1 * w1 + y2 * w2)


def _combine_call(dest, x, route, mod, ys, mc, sd, tc):
    m, d = x.shape
    midx = functools.partial(_mod_row_index, tm=tc, mc=mc, sd=sd)
    return pl.pallas_call(
        _combine_kernel,
        grid=(m // tc,),
        in_specs=[pl.BlockSpec((None, 1, TOP_K * tc), lambda i: (i, 0, 0), memory_space=pltpu.SMEM),
                  pl.BlockSpec((tc, d), lambda i: (i, 0)),
                  pl.BlockSpec((tc, LANES), lambda i: (i, 0)),
                  pl.BlockSpec((None, 6, d), lambda i: (midx(i), 0, 0)),
                  pl.BlockSpec(memory_space=pl.ANY)],
        out_specs=pl.BlockSpec((tc, d), lambda i: (i, 0)),
        out_shape=jax.ShapeDtypeStruct((m, d), F32),
        scratch_shapes=[pltpu.VMEM((TOP_K * tc * ROW_PITCH, LANES), F32), pltpu.SemaphoreType.DMA(())],
        compiler_params=_cparams("arbitrary"),
        name="moe_combine",
    )(dest, x, route, mod, ys)


def _moe_plan(route, n_experts, tm, n_tiles, tc, gather_rows):
    m = route.shape[0]
    e_flat = jnp.concatenate([route[:, 0], route[:, 1]]).astype(jnp.int32)
    onehot = (e_flat[:, None] == jnp.arange(n_experts, dtype=jnp.int32)[None, :]).astype(jnp.int32)
    csum = jnp.cumsum(onehot, axis=0)
    cnt = csum[-1]
    rank = jnp.take_along_axis(csum, e_flat[:, None], axis=1)[:, 0] - 1
    gsz = ((cnt + tm - 1) // tm) * tm
    off_end = jnp.cumsum(gsz)
    off = off_end - gsz
    dest = off[e_flat] + rank
    n_used = off_end[-1] // tm

    def owner(starts):
        return jnp.minimum(jnp.sum((off_end[None, :] <= starts[:, None]).astype(jnp.int32), axis=1), n_experts - 1)

    tiles = jnp.arange(n_tiles, dtype=jnp.int32)
    tile_e = owner(tiles * tm)
    tile_e = jnp.where(tiles < n_used, tile_e, tile_e[jnp.maximum(n_used - 1, 0)])
    blocks = jnp.arange(n_tiles * tm // gather_rows, dtype=jnp.int32) * gather_rows
    n_valid = jnp.where(blocks < off_end[-1], jnp.clip((off + cnt)[owner(blocks)] - blocks, 0, gather_rows), 0)
    tok = jnp.tile(jnp.arange(m, dtype=jnp.int32), TOP_K)
    src = jnp.zeros((n_tiles * tm,), jnp.int32).at[dest].set(tok)
    dest_tiles = jnp.concatenate([dest[:m].reshape(m // tc, 1, tc), dest[m:].reshape(m // tc, 1, tc)], axis=2)
    i32 = lambda a: a.astype(jnp.int32)
    return src, dest_tiles, i32(tile_e), i32(n_used.reshape(1)), i32(n_valid)


def _moe_layer(x, nw, mod, rw, wg, wu, wd, layer, mc, sd, tm_route, tm, tf, tc, gather_rows):
    m, d = x.shape
    n_experts = wg.shape[1]
    assert tm % gather_rows == 0
    n_tiles = -(-(TOP_K * m) // tm) + n_experts
    h_rows, route = _route_call(x, nw, mod, rw, mc, sd, tm_route, n_experts)
    src, dest_tiles, tile_e, n_used, n_valid = _moe_plan(route, n_experts, tm, n_tiles, tc, gather_rows)
    xs = _gather_call(h_rows, src, n_valid, gather_rows)
    ys = _experts_call(tile_e, n_used, xs, wg, wu, wd, layer, tm, tf)
    return _combine_call(dest_tiles, x, route, mod, ys, mc, sd, tc)


def _final_kernel(x_ref, w_ref, o_ref):
    o_ref[...] = _rms(x_ref[...], w_ref[...])


def _final_call(x, w, row0, rows, tm):
    d = x.shape[1]
    blk0 = row0 // tm
    return pl.pallas_call(
        _final_kernel,
        grid=(rows // tm,),
        in_specs=[pl.BlockSpec((tm, d), lambda i: (blk0 + i, 0)), pl.BlockSpec((1, d), lambda i: (0, 0))],
        out_specs=pl.BlockSpec((tm, d), lambda i: (i, 0)),
        out_shape=jax.ShapeDtypeStruct((rows, d), F32),
        compiler_params=_cparams("parallel"),
        name="final_norm",
    )(x, w)


def _sincos_2d(rows, cols, dim):
    quarter = dim // 4
    omega = 1.0 / (10000.0 ** (jnp.arange(quarter, dtype=F32) / quarter))
    r = jnp.arange(rows, dtype=F32)[:, None] * omega
    cc = jnp.arange(cols, dtype=F32)[:, None] * omega
    r_emb = jnp.concatenate([jnp.sin(r), jnp.cos(r)], axis=-1)
    c_emb = jnp.concatenate([jnp.sin(cc), jnp.cos(cc)], axis=-1)
    emb = jnp.concatenate([jnp.broadcast_to(r_emb[:, None, :], (rows, cols, dim // 2)),
                           jnp.broadcast_to(c_emb[None, :, :], (rows, cols, dim // 2))], axis=-1)
    return emb.reshape(rows * cols, dim)


def _pad_lanes(a):
    return jnp.pad(a, ((0, 0), (0, LANES - a.shape[1])))


def kernel(x_prompt, x_sample, c, state_delta, c_ctx, norm1_w, norm2_w, w_mod, b_mod, w_in, w_out, qkv_conv_w, delta_a_log, delta_dt_bias, delta_norm_w, sgu_norm_w, sgu_w, sgu_b, sconv_w, pool_w, pool_scale, ffn_w_gate, ffn_w_up, ffn_w_down, router_w, moe_w_gate, moe_w_up, moe_w_down, final_norm_w):
    bc, sc, d = x_prompt.shape
    bd, sd, _ = x_sample.shape
    depth = w_in.shape[0]
    mc, ml = bc * sc, bd * sd
    m = mc + ml
    w_a = H_A * LANES
    w_b, w_c, w_d = sgu_norm_w.shape[1], sconv_w.shape[1], pool_scale.shape[1]
    assert d == ROW_CHUNKS * LANES and mc % sd == 0 and w_a == w_b == w_c == w_d == d // 4
    assert delta_a_log.shape[1:] == (2, H_A) and delta_norm_w.shape[1] == LANES

    tm_mm = 512 if m % 512 == 0 else 256
    grid_w = 64
    pos = _sincos_2d(sd // grid_w, grid_w, d)
    x = jnp.concatenate([x_prompt.reshape(mc, d), (x_sample + pos[None]).reshape(ml, d)], axis=0)

    rm = -(-(1 + bd) // 8) * 8
    cond = jnp.zeros((rm, d), F32).at[0].set(c_ctx).at[1:1 + bd].set(c)
    mod_all = _mod_call(cond, w_mod, b_mod).reshape(depth, rm, 6, d)

    o0 = 3 * w_a
    o1 = o0 + w_a
    o3 = o1 + 4 * H_A
    o4 = o3 + 2 * w_b
    o5 = o4 + 3 * w_c
    blk0 = mc // sd
    w_main_all = jnp.concatenate([w_in[:, :, :o1], w_in[:, :, o3:]], axis=2).astype(BF16)
    w_ab_all = jnp.pad(w_in[:, :, o1:o3], ((0, 0), (0, 0), (0, LANES - (o3 - o1)))).astype(BF16)
    w_out_b = w_out.astype(BF16)
    ffn_b = [w.astype(BF16) for w in (ffn_w_gate, ffn_w_up, ffn_w_down)]
    moe_w = (moe_w_gate, moe_w_up, moe_w_down)
    ctx_states = []
    for l in range(depth):
        mod = mod_all[l]
        tm_in = 1024 if (mc % 1024 == 0 and sd % 1024 == 0) else tm_mm
        proj, ab = _inproj_call(x, norm1_w[l][None], mod, w_main_all, w_ab_all, l, mc, sd, tm_in, 1024)

        conv_t = qkv_conv_w[l].T
        alog = _pad_lanes(delta_a_log[l].reshape(1, 2 * H_A))
        dtb = _pad_lanes(delta_dt_bias[l].reshape(1, 2 * H_A))
        nwa = delta_norm_w[l][None]
        a_ctx, s_ctx = _delta_call(proj, ab, conv_t, alog, dtb, nwa, None, l,
                                   t_len=sc, nseq=bc, blk0=0, write_state=True)
        (a_lat,) = _delta_call(proj, ab, conv_t, alog, dtb, nwa, state_delta, l,
                               t_len=sd, nseq=bd, blk0=blk0, write_state=False)
        ctx_states.append(s_ctx)

        def both_paths(kern, name, col_block, width_in, params, pspecs, width_out):
            return (_seq_call(kern, name, proj, col_block, width_in, params, pspecs,
                              t_len=sc, nseq=bc, blk0=0, width_out=width_out),
                    _seq_call(kern, name, proj, col_block, width_in, params, pspecs,
                              t_len=sd, nseq=bd, blk0=blk0, width_out=width_out))

        n_g = sgu_w.shape[1]
        b_ctx, b_lat = both_paths(
            _sgu_kernel, "sgu", o1 // (2 * w_b), 2 * w_b,
            [sgu_norm_w[l][None], sgu_w[l].astype(BF16), sgu_b[l].reshape(n_g, SGU_CHUNK, 1)],
            [pl.BlockSpec((1, w_b), lambda b: (0, 0)),
             pl.BlockSpec((n_g, SGU_CHUNK, SGU_CHUNK), lambda b: (0, 0, 0)),
             pl.BlockSpec((n_g, SGU_CHUNK, 1), lambda b: (0, 0, 0))], w_b)
        c_ctx_mix, c_lat = both_paths(
            _sconv_kernel, "sconv", (o1 + 2 * w_b) // (3 * w_c), 3 * w_c,
            [sconv_w[l].T], [pl.BlockSpec((3, w_c), lambda b: (0, 0))], w_c)
        n_gd = pool_w.shape[1]
        d_ctx, d_lat = both_paths(
            _pool_kernel, "pool", (o1 + 2 * w_b + 3 * w_c) // w_d, w_d,
            [pool_w[l].astype(BF16), pool_scale[l][None]],
            [pl.BlockSpec((n_gd, LANES, LANES), lambda b: (0, 0, 0)),
             pl.BlockSpec((1, w_d), lambda b: (0, 0))], w_d)

        x = _outproj_call(x, (a_ctx, b_ctx, c_ctx_mix, d_ctx), (a_lat, b_lat, c_lat, d_lat),
                          w_out_b, l, mod, mc, sd, tm_mm)

        jl = l // 2
        if l % 2 == 0:
            x = _ffn_call(x, norm2_w[l][None], mod, *ffn_b, jl, mc, sd, tm_mm, 512)
        else:
            x = _moe_layer(x, norm2_w[l][None], mod, _pad_lanes(router_w[jl]), *moe_w, jl,
                           mc, sd, tm_route=256, tm=640, tf=512, tc=256, gather_rows=320)

    y_prompt = _final_call(x, final_norm_w[None], 0, mc, tm_mm).reshape(bc, sc, d)
    y_sample = _final_call(x, final_norm_w[None], mc, ml, tm_mm).reshape(bd, sd, d)
    new_state = jnp.stack(ctx_states, axis=1).astype(x_prompt.dtype)
    return (y_prompt, y_sample, new_state)
```

```python
import functools

import jax
import jax.numpy as jnp
from jax import lax
from jax.experimental import pallas as pl
from jax.experimental.pallas import tpu as pltpu

F32 = jnp.float32
BF16 = jnp.bfloat16

LANES = 128
ROW_CHUNKS = 16
ROW_PITCH = 24
DELTA_CHUNK = 64
SGU_CHUNK = 128
H_A = 4
POOL_WINDOWS = (2, 4, 8, 16)
TOP_K = 2
VMEM_LIMIT = 56 * 1024 * 1024


def _cparams(*sem):
    return pltpu.CompilerParams(dimension_semantics=sem, vmem_limit_bytes=VMEM_LIMIT)


def _silu(x):
    return x / (1.0 + jnp.exp(-x))


def _sigmoid(x):
    return 1.0 / (1.0 + jnp.exp(-x))


def _softplus(x):
    return jnp.maximum(x, 0.0) + jnp.log1p(jnp.exp(-jnp.abs(x)))


def _gelu_tanh(x):
    return 0.5 * x * (1.0 + jnp.tanh(0.7978845608028654 * (x + 0.044715 * (x * x * x))))


def _rms(x, w, eps=1e-6):
    return x * lax.rsqrt(jnp.mean(x * x, axis=-1, keepdims=True) + eps) * w


def _dot(a, b):
    return jnp.dot(a, b, preferred_element_type=F32)


def _split(a):
    hi = a.astype(BF16)
    lo = (a - hi.astype(F32)).astype(BF16)
    return hi, lo


def _dot3(a, b):
    a_hi, a_lo = _split(a)
    b_hi, b_lo = _split(b)
    return _dot(a_hi, b_hi) + _dot(a_hi, b_lo) + _dot(a_lo, b_hi)


def _bmm(a, b):
    return jnp.einsum('nij,njk->nik', a, b, preferred_element_type=F32)


def _bmm16(a, b):
    return _bmm(a.astype(BF16), b.astype(BF16))


def _mod_kernel(c_ref, w_ref, b_ref, o_ref):
    a = _silu(c_ref[...]).astype(BF16)
    o_ref[...] = _dot(a, w_ref[...].astype(BF16)) + b_ref[...]


def _mod_call(cond, w_mod, b_mod, tn=1024):
    depth, d, n = w_mod.shape
    rm = cond.shape[0]
    return pl.pallas_call(
        _mod_kernel,
        grid=(depth, n // tn),
        in_specs=[pl.BlockSpec((rm, d), lambda l, j: (0, 0)),
                  pl.BlockSpec((None, d, tn), lambda l, j: (l, 0, j)),
                  pl.BlockSpec((None, 1, tn), lambda l, j: (l, 0, j))],
        out_specs=pl.BlockSpec((None, rm, tn), lambda l, j: (l, 0, j)),
        out_shape=jax.ShapeDtypeStruct((depth, rm, n), F32),
        compiler_params=_cparams("parallel", "parallel"),
        name="adaln_mod",
    )(cond, w_mod, b_mod.reshape(depth, 1, n))


def _mod_row_index(i, tm, mc, sd):
    r0 = i * tm
    return jnp.where(r0 < mc, 0, 1 + (r0 - mc) // sd)


def _inproj_kernel(x_ref, nw_ref, mod_ref, wlo_ref, whi_ref, wab_ref, proj_ref, ab_ref, h_ref, *, n_lo_tiles):
    j = pl.program_id(1)

    @pl.when(j == 0)
    def _():
        h = _rms(x_ref[...], nw_ref[...]) * (1.0 + mod_ref[1:2, :]) + mod_ref[0:1, :]
        hb = h.astype(BF16)
        h_ref[...] = hb
        ab_ref[...] = _dot(hb, wab_ref[...])

    @pl.when(j < n_lo_tiles)
    def _():
        proj_ref[...] = _dot(h_ref[...], wlo_ref[...])

    @pl.when(j >= n_lo_tiles)
    def _():
        proj_ref[...] = _dot(h_ref[...], whi_ref[...])


def _inproj_call(x, nw, mod, w_lo, w_hi, w_ab, layer, mc, sd, tm, tn):
    m, d = x.shape
    n_lo, n_hi = w_lo.shape[2], w_hi.shape[2]
    nlt = n_lo // tn
    midx = functools.partial(_mod_row_index, tm=tm, mc=mc, sd=sd)
    return pl.pallas_call(
        functools.partial(_inproj_kernel, n_lo_tiles=nlt),
        grid=(m // tm, (n_lo + n_hi) // tn),
        in_specs=[pl.BlockSpec((tm, d), lambda i, j: (i, 0)),
                  pl.BlockSpec((1, d), lambda i, j: (0, 0)),
                  pl.BlockSpec((None, 6, d), lambda i, j: (midx(i), 0, 0)),
                  pl.BlockSpec((None, d, tn), lambda i, j: (layer, 0, jnp.minimum(j, nlt - 1))),
                  pl.BlockSpec((None, d, tn), lambda i, j: (layer, 0, jnp.maximum(j - nlt, 0))),
                  pl.BlockSpec((None, d, LANES), lambda i, j: (layer, 0, 0))],
        out_specs=[pl.BlockSpec((tm, tn), lambda i, j: (i, j)),
                   pl.BlockSpec((tm, LANES), lambda i, j: (i, 0))],
        out_shape=[jax.ShapeDtypeStruct((m, n_lo + n_hi), F32), jax.ShapeDtypeStruct((m, LANES), F32)],
        scratch_shapes=[pltpu.VMEM((tm, d), BF16)],
        compiler_params=_cparams("parallel", "arbitrary"),
        name="in_proj",
    )(x, nw, mod, w_lo, w_hi, w_ab)


def _shift_rows(x, d, row, t_len):
    if d == 0:
        return x
    y = pltpu.roll(x, (-d) % t_len, axis=0)
    ok = (row + d >= 0) & (row + d < t_len)
    return jnp.where(ok, y, 0.0)


def _conv3(x, w, row, t_len):
    return (_shift_rows(x, -1, row, t_len) * w[0:1, :] + x * w[1:2, :]
            + _shift_rows(x, 1, row, t_len) * w[2:3, :])


def _chunk_cumsum(g, row, t_len, reverse):
    pos = row % DELTA_CHUNK
    s = 1
    while s < DELTA_CHUNK:
        if reverse:
            g = g + jnp.where(pos < DELTA_CHUNK - s, pltpu.roll(g, t_len - s, axis=0), 0.0)
        else:
            g = g + jnp.where(pos >= s, pltpu.roll(g, s, axis=0), 0.0)
        s *= 2
    return g


def _unit_tri_inverse(lmat, r, c):
    eye = jnp.where(r == c, 1.0, 0.0)

    def same_block(b):
        return (r // b) == (c // b)

    x = jnp.where(same_block(8), -lmat, 0.0)
    x2 = _bmm16(x, x)
    x4 = _bmm16(x2, x2)
    p = _bmm16(_bmm16(eye + x, eye + x2), eye + x4)
    b = 8
    while b < DELTA_CHUNK:
        off = jnp.where(same_block(2 * b) & jnp.logical_not(same_block(b)), lmat, 0.0)
        pb = p.astype(BF16)
        p = p - _bmm16(_bmm16(pb, off), pb)
        b *= 2
    return p


def _delta_kernel(*refs, t_len, hps, has_s0, write_state):
    (q_ref, k_ref, v_ref, ga_ref, ab_ref, cq_ref, ck_ref, cv_ref, alog_ref, dtb_ref, nw_ref), rest = refs[:11], refs[11:]
    if has_s0:
        s0_ref, rest = rest[0], rest[1:]
    o_ref, rest = rest[0], rest[1:]
    if write_state:
        sout_ref = rest[0]

    n_ch = t_len // DELTA_CHUNK
    cl = DELTA_CHUNK
    row = lax.broadcasted_iota(jnp.int32, (t_len, LANES), 0)
    lane = lax.broadcasted_iota(jnp.int32, (t_len, LANES), 1)
    r64 = lax.broadcasted_iota(jnp.int32, (cl, cl), 0)
    c64 = lax.broadcasted_iota(jnp.int32, (cl, cl), 1)

    def l2n(x):
        return x * lax.rsqrt(jnp.sum(x * x, axis=-1, keepdims=True) + 1e-6)

    def column(a, idx):
        col = jnp.sum(jnp.where(lane == idx, a, 0.0), axis=1, keepdims=True)
        return jnp.broadcast_to(col, (t_len, LANES))

    def chunks(a):
        return a.reshape(n_ch, cl, LANES)

    ab = ab_ref[...]
    g_all = -jnp.exp(alog_ref[...]) * _softplus(ab + dtb_ref[...])
    b_all = _sigmoid(ab)

    chains = []
    for hh in range(hps):
        head = pl.program_id(1) * hps + hh
        cs = slice(hh * LANES, (hh + 1) * LANES)
        q = l2n(_silu(_conv3(q_ref[:, cs], cq_ref[:, cs], row, t_len))) * (LANES ** -0.5)
        k = l2n(_silu(_conv3(k_ref[:, cs], ck_ref[:, cs], row, t_len)))
        v = _silu(_conv3(v_ref[:, cs], cv_ref[:, cs], row, t_len))
        k3 = chunks(k)
        kb16 = k3.astype(BF16)
        kk = jnp.einsum('ncd,nsd->ncs', kb16, kb16, preferred_element_type=F32)
        qk = jnp.einsum('ncd,nsd->ncs', chunks(q).astype(BF16), kb16, preferred_element_type=F32)
        for direction in range(2):
            rev = direction == 1
            g = column(g_all, direction * H_A + head)
            beta = column(b_all, 2 * H_A + direction * H_A + head)
            gc = _chunk_cumsum(g, row, t_len, rev)
            gc3 = chunks(gc)
            gc_rows = jnp.swapaxes(gc3, 1, 2)[:, :cl, :]
            dmat = gc3[:, :, :cl] - gc_rows
            if rev:
                incl, strict = c64 >= r64, c64 > r64
            else:
                incl, strict = c64 <= r64, c64 < r64
            decay = jnp.exp(jnp.where(incl, dmat, 0.0))
            lmat = jnp.where(strict, chunks(beta)[:, :, :cl] * kk * decay, 0.0)
            tinv = _unit_tri_inverse(lmat, r64, c64).astype(BF16)
            egc = jnp.exp(gc)
            last = 0 if rev else cl - 1
            glast = gc3[:, last:last + 1, :]
            if has_s0:
                s_init = s0_ref[direction, hh]
            else:
                s_init = jnp.zeros((LANES, LANES), F32)
            uw = _bmm16(tinv, jnp.concatenate([chunks(v * beta), chunks(k * beta * egc)], axis=2))
            attn = jnp.where(incl, qk * decay, 0.0)
            kdt = jnp.swapaxes(k3 * jnp.exp(glast - gc3), 1, 2)
            chains.append(dict(
                hh=hh, direction=direction, s=s_init, outs=[None] * n_ch,
                order=list(range(n_ch - 1, -1, -1) if rev else range(n_ch)),
                u=uw[:, :, :LANES],
                w_qg=jnp.concatenate([uw[:, :, LANES:], chunks(q * egc)], axis=1).astype(BF16),
                attn_kdt=jnp.concatenate([attn, kdt], axis=1).astype(BF16),
                gl=jnp.exp(glast)))

    for t in range(n_ch):
        for ch in chains:
            n = ch['order'][t]
            ws_qs = _dot(ch['w_qg'][n], ch['s'].astype(BF16))
            v_new = ch['u'][n] - ws_qs[:cl]
            av_kv = _dot(ch['attn_kdt'][n], v_new.astype(BF16))
            ch['outs'][n] = ws_qs[cl:] + av_kv[:cl]
            ch['s'] = ch['s'] * ch['gl'][n] + av_kv[cl:]

    for ch in chains:
        if write_state:
            sout_ref[ch['direction'], ch['hh']] = ch['s']
    for hh in range(hps):
        cs = slice(hh * LANES, (hh + 1) * LANES)
        fwd, bwd = chains[2 * hh], chains[2 * hh + 1]
        o = jnp.concatenate([a + b for a, b in zip(fwd['outs'], bwd['outs'])], axis=0)
        o_ref[:, cs] = (_rms(o, nw_ref[...]) * _silu(ga_ref[:, cs])).astype(o_ref.dtype)


def _delta_call(proj, ab, conv_t, alog, dtb, nw, s0, layer, *, t_len, nseq, blk0, write_state, hps=2):
    has_s0 = s0 is not None
    w_a = H_A * LANES
    wh = hps * LANES
    nhb = H_A // hps

    def pspec(part):
        return pl.BlockSpec((t_len, wh), lambda b, h: (blk0 + b, part * nhb + h))

    def cspec(part):
        return pl.BlockSpec((3, wh), lambda b, h: (0, part * nhb + h))

    row1 = pl.BlockSpec((1, LANES), lambda b, h: (0, 0))
    in_specs = [pspec(0), pspec(1), pspec(2), pspec(3),
                pl.BlockSpec((t_len, LANES), lambda b, h: (blk0 + b, 0)),
                cspec(0), cspec(1), cspec(2), row1, row1, row1]
    args = [proj, proj, proj, proj, ab, conv_t, conv_t, conv_t, alog, dtb, nw]
    if has_s0:
        in_specs.append(pl.BlockSpec((None, None, 2, hps, LANES, LANES), lambda b, h: (b, layer, 0, h, 0, 0)))
        args.append(s0)
    out_specs = [pl.BlockSpec((t_len, wh), lambda b, h: (b, h))]
    out_shape = [jax.ShapeDtypeStruct((nseq * t_len, w_a), BF16)]
    if write_state:
        out_specs.append(pl.BlockSpec((None, 2, hps, LANES, LANES), lambda b, h: (b, 0, h, 0, 0)))
        out_shape.append(jax.ShapeDtypeStruct((nseq, 2, H_A, LANES, LANES), F32))

    return pl.pallas_call(
        functools.partial(_delta_kernel, t_len=t_len, hps=hps, has_s0=has_s0, write_state=write_state),
        grid=(nseq, nhb),
        in_specs=in_specs,
        out_specs=out_specs,
        out_shape=out_shape,
        compiler_params=_cparams("parallel", "parallel"),
        name=f"delta_T{t_len}",
    )(*args)


def _sgu_kernel(z_ref, nw_ref, w_ref, b_ref, o_ref, *, t_len):
    wb = o_ref.shape[1]
    z = _gelu_tanh(z_ref[...])
    u, v = z[:, :wb], z[:, wb:]
    mu = jnp.mean(v, axis=-1, keepdims=True)
    vc = v - mu
    vn = (vc * lax.rsqrt(jnp.mean(vc * vc, axis=-1, keepdims=True) + 1e-5) * nw_ref[...]).astype(BF16)
    for n in range(t_len // SGU_CHUNK):
        rs = slice(n * SGU_CHUNK, (n + 1) * SGU_CHUNK)
        for g in range(wb // LANES):
            cs = slice(g * LANES, (g + 1) * LANES)
            sp = _dot(w_ref[g], vn[rs, cs]) + b_ref[g]
            o_ref[rs, cs] = (u[rs, cs] * sp).astype(o_ref.dtype)


def _sconv_kernel(c_ref, w_ref, o_ref, *, t_len):
    wc = o_ref.shape[1]
    row = lax.broadcasted_iota(jnp.int32, (t_len, wc), 0)
    x = c_ref[...]
    y = _conv3(x[:, wc:2 * wc] * x[:, 2 * wc:], w_ref[...], row, t_len)
    o_ref[...] = (x[:, :wc] * y).astype(o_ref.dtype)


def _pool_kernel(p_ref, w_ref, sc_ref, o_ref, *, t_len):
    row = lax.broadcasted_iota(jnp.int32, (t_len, LANES), 0)
    for j, win in enumerate(POOL_WINDOWS):
        cs = slice(j * LANES, (j + 1) * LANES)
        x = p_ref[:, cs]
        half = win // 2
        acc = x
        for d in range(-half, win - half):
            if d != 0:
                acc = acc + _shift_rows(x, d, row, t_len)
        cnt = (jnp.minimum(row + (win - half), t_len) - jnp.maximum(row - half, 0)).astype(F32)
        pooled = acc / cnt - x
        o_ref[:, cs] = (_dot(pooled.astype(BF16), w_ref[j]) * sc_ref[:, cs]).astype(o_ref.dtype)


def _seq_call(kern, name, proj, col_block, width_in, params, pspecs, *, t_len, nseq, blk0, width_out):
    return pl.pallas_call(
        functools.partial(kern, t_len=t_len),
        grid=(nseq,),
        in_specs=[pl.BlockSpec((t_len, width_in), lambda b: (blk0 + b, col_block))] + pspecs,
        out_specs=pl.BlockSpec((t_len, width_out), lambda b: (b, 0)),
        out_shape=jax.ShapeDtypeStruct((nseq * t_len, width_out), BF16),
        compiler_params=_cparams("parallel"),
        name=f"{name}_T{t_len}",
    )(proj, *params)


def _outproj_kernel(x_ref, *refs, n_ctx_tiles):
    n_mix = (len(refs) - 3) // 2
    ctx_refs, lat_refs = refs[:n_mix], refs[n_mix:2 * n_mix]
    w_ref, mod_ref, o_ref = refs[2 * n_mix:]

    def run(mix_refs):
        acc = None
        for i, m_ref in enumerate(mix_refs):
            wq = m_ref.shape[1]
            part = _dot(m_ref[...], w_ref[i * wq:(i + 1) * wq, :])
            acc = part if acc is None else acc + part
        o_ref[...] = x_ref[...] + mod_ref[2:3, :] * acc

    is_ctx = pl.program_id(0) < n_ctx_tiles
    pl.when(is_ctx)(lambda: run(ctx_refs))
    pl.when(jnp.logical_not(is_ctx))(lambda: run(lat_refs))


def _outproj_call(x, mixes_ctx, mixes_lat, w_out, layer, mod, mc, sd, tm):
    m, d = x.shape
    wq = mixes_ctx[0].shape[1]
    nct = mc // tm
    midx = functools.partial(_mod_row_index, tm=tm, mc=mc, sd=sd)
    cspec = pl.BlockSpec((tm, wq), lambda i: (jnp.minimum(i, nct - 1), 0))
    lspec = pl.BlockSpec((tm, wq), lambda i: (jnp.maximum(i - nct, 0), 0))
    n_mix = len(mixes_ctx)
    return pl.pallas_call(
        functools.partial(_outproj_kernel, n_ctx_tiles=nct),
        grid=(m // tm,),
        in_specs=[pl.BlockSpec((tm, d), lambda i: (i, 0))] + [cspec] * n_mix + [lspec] * n_mix
                 + [pl.BlockSpec((None, d, d), lambda i: (layer, 0, 0)),
                    pl.BlockSpec((None, 6, d), lambda i: (midx(i), 0, 0))],
        out_specs=pl.BlockSpec((tm, d), lambda i: (i, 0)),
        out_shape=jax.ShapeDtypeStruct((m, d), F32),
        compiler_params=_cparams("parallel"),
        name="out_proj",
    )(x, *mixes_ctx, *mixes_lat, w_out, mod)


def _ffn_kernel(x_ref, nw_ref, mod_ref, wg_ref, wu_ref, wd_ref, o_ref, h_ref, acc_ref):
    j = pl.program_id(1)

    @pl.when(j == 0)
    def _():
        h = _rms(x_ref[...], nw_ref[...]) * (1.0 + mod_ref[4:5, :]) + mod_ref[3:4, :]
        h_ref[...] = h.astype(BF16)
        acc_ref[...] = jnp.zeros_like(acc_ref)

    h = h_ref[...]
    a = (_silu(_dot(h, wg_ref[...])) * _dot(h, wu_ref[...])).astype(BF16)
    acc_ref[...] += _dot(a, wd_ref[...])

    @pl.when(j == pl.num_programs(1) - 1)
    def _():
        o_ref[...] = x_ref[...] + mod_ref[5:6, :] * acc_ref[...]


def _ffn_call(x, nw, mod, wg, wu, wd, layer, mc, sd, tm, tf):
    m, d = x.shape
    f = wg.shape[2]
    midx = functools.partial(_mod_row_index, tm=tm, mc=mc, sd=sd)
    return pl.pallas_call(
        _ffn_kernel,
        grid=(m // tm, f // tf),
        in_specs=[pl.BlockSpec((tm, d), lambda i, j: (i, 0)),
                  pl.BlockSpec((1, d), lambda i, j: (0, 0)),
                  pl.BlockSpec((None, 6, d), lambda i, j: (midx(i), 0, 0)),
                  pl.BlockSpec((None, d, tf), lambda i, j: (layer, 0, j)),
                  pl.BlockSpec((None, d, tf), lambda i, j: (layer, 0, j)),
                  pl.BlockSpec((None, tf, d), lambda i, j: (layer, j, 0))],
        out_specs=pl.BlockSpec((tm, d), lambda i, j: (i, 0)),
        out_shape=jax.ShapeDtypeStruct((m, d), F32),
        scratch_shapes=[pltpu.VMEM((tm, d), BF16), pltpu.VMEM((tm, d), F32)],
        compiler_params=_cparams("parallel", "arbitrary"),
        name="ffn_dense",
    )(x, nw, mod, wg, wu, wd)


def _route_kernel(x_ref, nw_ref, mod_ref, rw_ref, h_ref, r_ref, *, n_experts):
    tm = x_ref.shape[0]
    h = _rms(x_ref[...], nw_ref[...]) * (1.0 + mod_ref[4:5, :]) + mod_ref[3:4, :]
    for cb in range(ROW_CHUNKS):
        h_ref[pl.ds(cb, tm, stride=ROW_CHUNKS), :] = h[:, cb * LANES:(cb + 1) * LANES]
    lane = lax.broadcasted_iota(jnp.int32, (tm, LANES), 1).astype(F32)
    neg = jnp.float32(-jnp.inf)
    logits = jnp.where(lane < n_experts, _dot3(h, rw_ref[...]), neg)
    m1 = jnp.max(logits, axis=-1, keepdims=True)
    i1 = jnp.min(jnp.where(logits == m1, lane, float(LANES)), axis=-1, keepdims=True)
    rest = jnp.where(lane == i1, neg, logits)
    m2 = jnp.max(rest, axis=-1, keepdims=True)
    i2 = jnp.min(jnp.where(rest == m2, lane, float(LANES)), axis=-1, keepdims=True)
    e2 = jnp.exp(m2 - m1)
    den = 1.0 + e2
    r_ref[...] = jnp.where(lane == 0.0, i1,
                           jnp.where(lane == 1.0, i2,
                                     jnp.where(lane == 2.0, 1.0 / den, jnp.where(lane == 3.0, e2 / den, 0.0))))


def _route_call(x, nw, mod, rw, mc, sd, tm, n_experts):
    m, d = x.shape
    midx = functools.partial(_mod_row_index, tm=tm, mc=mc, sd=sd)
    return pl.pallas_call(
        functools.partial(_route_kernel, n_experts=n_experts),
        grid=(m // tm,),
        in_specs=[pl.BlockSpec((tm, d), lambda i: (i, 0)),
                  pl.BlockSpec((1, d), lambda i: (0, 0)),
                  pl.BlockSpec((None, 6, d), lambda i: (midx(i), 0, 0)),
                  pl.BlockSpec((d, LANES), lambda i: (0, 0))],
        out_specs=[pl.BlockSpec((tm * ROW_CHUNKS, LANES), lambda i: (i, 0)),
                   pl.BlockSpec((tm, LANES), lambda i: (i, 0))],
        out_shape=[jax.ShapeDtypeStruct((m * ROW_CHUNKS, LANES), F32),
                   jax.ShapeDtypeStruct((m, LANES), F32)],
        compiler_params=_cparams("parallel"),
        name="moe_route",
    )(x, nw, mod, rw)


def _row_copy(src_hbm, dst_ref, src_row, dst_row, sem):
    s0 = pl.multiple_of(src_row * ROW_CHUNKS, ROW_CHUNKS)
    d0 = pl.multiple_of(dst_row * ROW_PITCH, 8)
    return pltpu.make_async_copy(src_hbm.at[pl.ds(s0, ROW_CHUNKS), :], dst_ref.at[pl.ds(d0, ROW_CHUNKS), :], sem)


def _gather_kernel(nv_ref, idx_ref, src_hbm, o_ref, buf_ref, sem, *, rows):
    blk = pl.program_id(0)
    n_valid = nv_ref[blk]

    @pl.when(blk == 0)
    def _():
        buf_ref[...] = jnp.zeros_like(buf_ref)

    def start(r, carry):
        _row_copy(src_hbm, buf_ref, idx_ref[0, r], r, sem).start()
        return carry

    def wait(r, carry):
        _row_copy(src_hbm, buf_ref, idx_ref[0, r], r, sem).wait()
        return carry

    lax.fori_loop(0, n_valid, start, 0)
    lax.fori_loop(0, n_valid, wait, 0)
    keep = lax.broadcasted_iota(jnp.int32, (rows, LANES), 0) < n_valid
    for cb in range(ROW_CHUNKS):
        val = buf_ref[pl.ds(cb, rows, stride=ROW_PITCH), :]
        o_ref[:, cb * LANES:(cb + 1) * LANES] = jnp.where(keep, val, 0.0).astype(o_ref.dtype)


def _gather_call(h_rows, src_idx, n_valid, rows):
    p = src_idx.shape[0]
    d = ROW_CHUNKS * LANES
    return pl.pallas_call(
        functools.partial(_gather_kernel, rows=rows),
        grid_spec=pltpu.PrefetchScalarGridSpec(
            num_scalar_prefetch=1,
            grid=(p // rows,),
            in_specs=[pl.BlockSpec((None, 1, rows), lambda i, n: (i, 0, 0), memory_space=pltpu.SMEM),
                      pl.BlockSpec(memory_space=pl.ANY)],
            out_specs=pl.BlockSpec((rows, d), lambda i, n: (i, 0)),
            scratch_shapes=[pltpu.VMEM((rows * ROW_PITCH, LANES), F32), pltpu.SemaphoreType.DMA(())]),
        out_shape=jax.ShapeDtypeStruct((p, d), BF16),
        compiler_params=_cparams("arbitrary"),
        name="moe_gather",
    )(n_valid, src_idx.reshape(p // rows, 1, rows), h_rows)


def _experts_kernel(te_ref, nt_ref, x_ref, wg_ref, wu_ref, wd_ref, o_ref, acc_ref):
    i, j = pl.program_id(0), pl.program_id(1)
    tm = x_ref.shape[0]
    last_j = pl.num_programs(1) - 1

    @pl.when(i < nt_ref[0])
    def _():
        x = x_ref[...]
        a = (_silu(_dot(x, wg_ref[...].astype(BF16))) * _dot(x, wu_ref[...].astype(BF16))).astype(BF16)
        part = _dot(a, wd_ref[...].astype(BF16))

        @pl.when(j == 0)
        def _():
            acc_ref[...] = part

        @pl.when(j != 0)
        def _():
            acc_ref[...] += part

        @pl.when(j == last_j)
        def _():
            for cb in range(ROW_CHUNKS):
                o_ref[pl.ds(cb, tm, stride=ROW_CHUNKS), :] = acc_ref[:, cb * LANES:(cb + 1) * LANES]

    @pl.when((i >= nt_ref[0]) & (j == last_j))
    def _():
        o_ref[...] = jnp.zeros_like(o_ref)


def _experts_call(tile_e, ntiles, xs, wg, wu, wd, layer, tm, tf):
    d, f = wg.shape[2:]
    p = xs.shape[0]
    nf = f // tf

    def row_blk(i, j, te, nt):
        return (jnp.minimum(i, nt[0] - 1), 0)

    def jj(i, j, nt):
        return jnp.where(i < nt[0], j, nf - 1)

    def w_in_blk(i, j, te, nt):
        return (layer, te[i], 0, jj(i, j, nt))

    def w_out_blk(i, j, te, nt):
        return (layer, te[i], jj(i, j, nt), 0)

    return pl.pallas_call(
        _experts_kernel,
        grid_spec=pltpu.PrefetchScalarGridSpec(
            num_scalar_prefetch=2,
            grid=(p // tm, nf),
            in_specs=[pl.BlockSpec((tm, d), row_blk),
                      pl.BlockSpec((None, None, d, tf), w_in_blk),
                      pl.BlockSpec((None, None, d, tf), w_in_blk),
                      pl.BlockSpec((None, None, tf, d), w_out_blk)],
            out_specs=pl.BlockSpec((tm * ROW_CHUNKS, LANES), lambda i, j, te, nt: (i, 0)),
            scratch_shapes=[pltpu.VMEM((tm, d), F32)]),
        out_shape=jax.ShapeDtypeStruct((p * ROW_CHUNKS, LANES), F32),
        compiler_params=_cparams("arbitrary", "arbitrary"),
        name="moe_experts",
    )(tile_e, ntiles, xs, wg, wu, wd)


def _combine_kernel(idx_ref, x_ref, r_ref, mod_ref, ys_hbm, o_ref, buf_ref, sem):
    tc = x_ref.shape[0]
    n_rows = TOP_K * tc

    def start(r, carry):
        _row_copy(ys_hbm, buf_ref, idx_ref[0, r], r, sem).start()
        return carry

    def wait(r, carry):
        _row_copy(ys_hbm, buf_ref, idx_ref[0, r], r, sem).wait()
        return carry

    lax.fori_loop(0, n_rows, start, 0, unroll=8)
    lax.fori_loop(0, n_rows, wait, 0, unroll=8)
    w1 = r_ref[:, 2:3]
    w2 = r_ref[:, 3:4]
    for cb in range(ROW_CHUNKS):
        cs = slice(cb * LANES, (cb + 1) * LANES)
        y1 = buf_ref[pl.ds(cb, tc, stride=ROW_PITCH), :]
        y2 = buf_ref[pl.ds(tc * ROW_PITCH + cb, tc, stride=ROW_PITCH), :]
        o_ref[:, cs] = x_ref[:, cs] + mod_ref[5:6, cs] * (y1 * w1 + y2 * w2)


def _combine_call(dest, x, route, mod, ys, mc, sd, tc):
    m, d = x.shape
    midx = functools.partial(_mod_row_index, tm=tc, mc=mc, sd=sd)
    return pl.pallas_call(
        _combine_kernel,
        grid=(m // tc,),
        in_specs=[pl.BlockSpec((None, 1, TOP_K * tc), lambda i: (i, 0, 0), memory_space=pltpu.SMEM),
                  pl.BlockSpec((tc, d), lambda i: (i, 0)),
                  pl.BlockSpec((tc, LANES), lambda i: (i, 0)),
                  pl.BlockSpec((None, 6, d), lambda i: (midx(i), 0, 0)),
                  pl.BlockSpec(memory_space=pl.ANY)],
        out_specs=pl.BlockSpec((tc, d), lambda i: (i, 0)),
        out_shape=jax.ShapeDtypeStruct((m, d), F32),
        scratch_shapes=[pltpu.VMEM((TOP_K * tc * ROW_PITCH, LANES), F32), pltpu.SemaphoreType.DMA(())],
        compiler_params=_cparams("arbitrary"),
        name="moe_combine",
    )(dest, x, route, mod, ys)


def _moe_plan(route, n_experts, tm, n_tiles, tc, gather_rows):
    m = route.shape[0]
    e_flat = jnp.concatenate([route[:, 0], route[:, 1]]).astype(jnp.int32)
    onehot = (e_flat[:, None] == jnp.arange(n_experts, dtype=jnp.int32)[None, :]).astype(jnp.int32)
    csum = jnp.cumsum(onehot, axis=0)
    cnt = csum[-1]
    rank = jnp.take_along_axis(csum, e_flat[:, None], axis=1)[:, 0] - 1
    gsz = ((cnt + tm - 1) // tm) * tm
    off_end = jnp.cumsum(gsz)
    off = off_end - gsz
    dest = off[e_flat] + rank
    n_used = off_end[-1] // tm

    def owner(starts):
        return jnp.minimum(jnp.sum((off_end[None, :] <= starts[:, None]).astype(jnp.int32), axis=1), n_experts - 1)

    tiles = jnp.arange(n_tiles, dtype=jnp.int32)
    tile_e = owner(tiles * tm)
    tile_e = jnp.where(tiles < n_used, tile_e, tile_e[jnp.maximum(n_used - 1, 0)])
    blocks = jnp.arange(n_tiles * tm // gather_rows, dtype=jnp.int32) * gather_rows
    n_valid = jnp.where(blocks < off_end[-1], jnp.clip((off + cnt)[owner(blocks)] - blocks, 0, gather_rows), 0)
    tok = jnp.tile(jnp.arange(m, dtype=jnp.int32), TOP_K)
    src = jnp.zeros((n_tiles * tm,), jnp.int32).at[dest].set(tok)
    dest_tiles = jnp.concatenate([dest[:m].reshape(m // tc, 1, tc), dest[m:].reshape(m // tc, 1, tc)], axis=2)
    i32 = lambda a: a.astype(jnp.int32)
    return src, dest_tiles, i32(tile_e), i32(n_used.reshape(1)), i32(n_valid)


def _moe_layer(x, nw, mod, rw, wg, wu, wd, layer, mc, sd, tm_route, tm, tf, tc, gather_rows):
    m, d = x.shape
    n_experts = wg.shape[1]
    assert tm % gather_rows == 0
    n_tiles = -(-(TOP_K * m) // tm) + n_experts
    h_rows, route = _route_call(x, nw, mod, rw, mc, sd, tm_route, n_experts)
    src, dest_tiles, tile_e, n_used, n_valid = _moe_plan(route, n_experts, tm, n_tiles, tc, gather_rows)
    xs = _gather_call(h_rows, src, n_valid, gather_rows)
    ys = _experts_call(tile_e, n_used, xs, wg, wu, wd, layer, tm, tf)
    return _combine_call(dest_tiles, x, route, mod, ys, mc, sd, tc)


def _final_kernel(x_ref, w_ref, o_ref):
    o_ref[...] = _rms(x_ref[...], w_ref[...])


def _final_call(x, w, row0, rows, tm):
    d = x.shape[1]
    blk0 = row0 // tm
    return pl.pallas_call(
        _final_kernel,
        grid=(rows // tm,),
        in_specs=[pl.BlockSpec((tm, d), lambda i: (blk0 + i, 0)), pl.BlockSpec((1, d), lambda i: (0, 0))],
        out_specs=pl.BlockSpec((tm, d), lambda i: (i, 0)),
        out_shape=jax.ShapeDtypeStruct((rows, d), F32),
        compiler_params=_cparams("parallel"),
        name="final_norm",
    )(x, w)


def _sincos_2d(rows, cols, dim):
    quarter = dim // 4
    omega = 1.0 / (10000.0 ** (jnp.arange(quarter, dtype=F32) / quarter))
    r = jnp.arange(rows, dtype=F32)[:, None] * omega
    cc = jnp.arange(cols, dtype=F32)[:, None] * omega
    r_emb = jnp.concatenate([jnp.sin(r), jnp.cos(r)], axis=-1)
    c_emb = jnp.concatenate([jnp.sin(cc), jnp.cos(cc)], axis=-1)
    emb = jnp.concatenate([jnp.broadcast_to(r_emb[:, None, :], (rows, cols, dim // 2)),
                           jnp.broadcast_to(c_emb[None, :, :], (rows, cols, dim // 2))], axis=-1)
    return emb.reshape(rows * cols, dim)


def _pad_lanes(a):
    return jnp.pad(a, ((0, 0), (0, LANES - a.shape[1])))


def kernel(x_prompt, x_sample, c, state_delta, c_ctx, norm1_w, norm2_w, w_mod, b_mod, w_in, w_out, qkv_conv_w, delta_a_log, delta_dt_bias, delta_norm_w, sgu_norm_w, sgu_w, sgu_b, sconv_w, pool_w, pool_scale, ffn_w_gate, ffn_w_up, ffn_w_down, router_w, moe_w_gate, moe_w_up, moe_w_down, final_norm_w):
    bc, sc, d = x_prompt.shape
    bd, sd, _ = x_sample.shape
    depth = w_in.shape[0]
    mc, ml = bc * sc, bd * sd
    m = mc + ml
    w_a = H_A * LANES
    w_b, w_c, w_d = sgu_norm_w.shape[1], sconv_w.shape[1], pool_scale.shape[1]
    assert d == ROW_CHUNKS * LANES and mc % sd == 0 and w_a == w_b == w_c == w_d == d // 4
    assert delta_a_log.shape[1:] == (2, H_A) and delta_norm_w.shape[1] == LANES

    tm_mm = 512 if m % 512 == 0 else 256
    grid_w = 64
    pos = _sincos_2d(sd // grid_w, grid_w, d)
    x = jnp.concatenate([x_prompt.reshape(mc, d), (x_sample + pos[None]).reshape(ml, d)], axis=0)

    rm = -(-(1 + bd) // 8) * 8
    cond = jnp.zeros((rm, d), F32).at[0].set(c_ctx).at[1:1 + bd].set(c)
    mod_all = _mod_call(cond, w_mod, b_mod).reshape(depth, rm, 6, d)

    o0 = 3 * w_a
    o1 = o0 + w_a
    o3 = o1 + 4 * H_A
    o4 = o3 + 2 * w_b
    o5 = o4 + 3 * w_c
    blk0 = mc // sd
    w_lo_all = w_in[:, :, :o1].astype(BF16)
    w_hi_all = w_in[:, :, o3:].astype(BF16)
    w_ab_all = jnp.pad(w_in[:, :, o1:o3], ((0, 0), (0, 0), (0, LANES - (o3 - o1)))).astype(BF16)
    w_out_b = w_out.astype(BF16)
    ffn_b = [w.astype(BF16) for w in (ffn_w_gate, ffn_w_up, ffn_w_down)]
    moe_w = (moe_w_gate, moe_w_up, moe_w_down)
    ctx_states = []
    for l in range(depth):
        mod = mod_all[l]
        tm_in = 1024 if (mc % 1024 == 0 and sd % 1024 == 0) else tm_mm
        proj, ab = _inproj_call(x, norm1_w[l][None], mod, w_lo_all, w_hi_all, w_ab_all, l, mc, sd, tm_in, 1024)

        conv_t = qkv_conv_w[l].T
        alog = _pad_lanes(delta_a_log[l].reshape(1, 2 * H_A))
        dtb = _pad_lanes(delta_dt_bias[l].reshape(1, 2 * H_A))
        nwa = delta_norm_w[l][None]
        a_ctx, s_ctx = _delta_call(proj, ab, conv_t, alog, dtb, nwa, None, l,
                                   t_len=sc, nseq=bc, blk0=0, write_state=True)
        (a_lat,) = _delta_call(proj, ab, conv_t, alog, dtb, nwa, state_delta, l,
                               t_len=sd, nseq=bd, blk0=blk0, write_state=False)
        ctx_states.append(s_ctx)

        def both_paths(kern, name, col_block, width_in, params, pspecs, width_out):
            return (_seq_call(kern, name, proj, col_block, width_in, params, pspecs,
                              t_len=sc, nseq=bc, blk0=0, width_out=width_out),
                    _seq_call(kern, name, proj, col_block, width_in, params, pspecs,
                              t_len=sd, nseq=bd, blk0=blk0, width_out=width_out))

        n_g = sgu_w.shape[1]
        b_ctx, b_lat = both_paths(
            _sgu_kernel, "sgu", o1 // (2 * w_b), 2 * w_b,
            [sgu_norm_w[l][None], sgu_w[l].astype(BF16), sgu_b[l].reshape(n_g, SGU_CHUNK, 1)],
            [pl.BlockSpec((1, w_b), lambda b: (0, 0)),
             pl.BlockSpec((n_g, SGU_CHUNK, SGU_CHUNK), lambda b: (0, 0, 0)),
             pl.BlockSpec((n_g, SGU_CHUNK, 1), lambda b: (0, 0, 0))], w_b)
        c_ctx_mix, c_lat = both_paths(
            _sconv_kernel, "sconv", (o1 + 2 * w_b) // (3 * w_c), 3 * w_c,
            [sconv_w[l].T], [pl.BlockSpec((3, w_c), lambda b: (0, 0))], w_c)
        n_gd = pool_w.shape[1]
        d_ctx, d_lat = both_paths(
            _pool_kernel, "pool", (o1 + 2 * w_b + 3 * w_c) // w_d, w_d,
            [pool_w[l].astype(BF16), pool_scale[l][None]],
            [pl.BlockSpec((n_gd, LANES, LANES), lambda b: (0, 0, 0)),
             pl.BlockSpec((1, w_d), lambda b: (0, 0))], w_d)

        x = _outproj_call(x, (a_ctx, b_ctx, c_ctx_mix, d_ctx), (a_lat, b_lat, c_lat, d_lat),
                          w_out_b, l, mod, mc, sd, tm_mm)

        jl = l // 2
        if l % 2 == 0:
            x = _ffn_call(x, norm2_w[l][None], mod, *ffn_b, jl, mc, sd, tm_mm, 512)
        else:
            x = _moe_layer(x, norm2_w[l][None], mod, _pad_lanes(router_w[jl]), *moe_w, jl,
                           mc, sd, tm_route=256, tm=640, tf=512, tc=256, gather_rows=640)

    y_prompt = _final_call(x, final_norm_w[None], 0, mc, tm_mm).reshape(bc, sc, d)
    y_sample = _final_call(x, final_norm_w[None], mc, ml, tm_mm).reshape(bd, sd, d)
    new_state = jnp.stack(ctx_states, axis=1).astype(x_prompt.dtype)
    return (y_prompt, y_sample, new_state)
```

```python
import functools

import jax
import jax.numpy as jnp
from jax import lax
from jax.experimental import pallas as pl
from jax.experimental.pallas import tpu as pltpu

F32 = jnp.float32
BF16 = jnp.bfloat16

LANES = 128
ROW_CHUNKS = 16
ROW_PITCH = 24
DELTA_CHUNK = 64
SGU_CHUNK = 128
H_A = 4
POOL_WINDOWS = (2, 4, 8, 16)
TOP_K = 2
VMEM_LIMIT = 56 * 1024 * 1024


def _cparams(*sem):
    return pltpu.CompilerParams(dimension_semantics=sem, vmem_limit_bytes=VMEM_LIMIT)


def _silu(x):
    return x / (1.0 + jnp.exp(-x))


def _sigmoid(x):
    return 1.0 / (1.0 + jnp.exp(-x))


def _softplus(x):
    return jnp.maximum(x, 0.0) + jnp.log1p(jnp.exp(-jnp.abs(x)))


def _gelu_tanh(x):
    return 0.5 * x * (1.0 + jnp.tanh(0.7978845608028654 * (x + 0.044715 * (x * x * x))))


def _rms(x, w, eps=1e-6):
    return x * lax.rsqrt(jnp.mean(x * x, axis=-1, keepdims=True) + eps) * w


def _dot(a, b):
    return jnp.dot(a, b, preferred_element_type=F32)


def _split(a):
    hi = a.astype(BF16)
    lo = (a - hi.astype(F32)).astype(BF16)
    return hi, lo


def _dot3(a, b):
    a_hi, a_lo = _split(a)
    b_hi, b_lo = _split(b)
    return _dot(a_hi, b_hi) + _dot(a_hi, b_lo) + _dot(a_lo, b_hi)


def _bmm(a, b):
    return jnp.einsum('nij,njk->nik', a, b, preferred_element_type=F32)


def _bmm16(a, b):
    return _bmm(a.astype(BF16), b.astype(BF16))


def _mod_kernel(c_ref, w_ref, b_ref, o_ref):
    a = _silu(c_ref[...]).astype(BF16)
    o_ref[...] = _dot(a, w_ref[...].astype(BF16)) + b_ref[...]


def _mod_call(cond, w_mod, b_mod, tn=1024):
    depth, d, n = w_mod.shape
    rm = cond.shape[0]
    return pl.pallas_call(
        _mod_kernel,
        grid=(depth, n // tn),
        in_specs=[pl.BlockSpec((rm, d), lambda l, j: (0, 0)),
                  pl.BlockSpec((None, d, tn), lambda l, j: (l, 0, j)),
                  pl.BlockSpec((None, 1, tn), lambda l, j: (l, 0, j))],
        out_specs=pl.BlockSpec((None, rm, tn), lambda l, j: (l, 0, j)),
        out_shape=jax.ShapeDtypeStruct((depth, rm, n), F32),
        compiler_params=_cparams("parallel", "parallel"),
        name="adaln_mod",
    )(cond, w_mod, b_mod.reshape(depth, 1, n))


def _mod_row_index(i, tm, mc, sd):
    r0 = i * tm
    return jnp.where(r0 < mc, 0, 1 + (r0 - mc) // sd)


def _inproj_kernel(x_ref, nw_ref, mod_ref, w_ref, wab_ref, proj_ref, ab_ref, h_ref):
    @pl.when(pl.program_id(1) == 0)
    def _():
        h = _rms(x_ref[...], nw_ref[...]) * (1.0 + mod_ref[1:2, :]) + mod_ref[0:1, :]
        hb = h.astype(BF16)
        h_ref[...] = hb
        ab_ref[...] = _dot(hb, wab_ref[...])

    proj_ref[...] = _dot(h_ref[...], w_ref[...])


def _inproj_call(x, nw, mod, w_main, w_ab, layer, mc, sd, tm, tn):
    m, d = x.shape
    n = w_main.shape[2]
    midx = functools.partial(_mod_row_index, tm=tm, mc=mc, sd=sd)
    return pl.pallas_call(
        _inproj_kernel,
        grid=(m // tm, n // tn),
        in_specs=[pl.BlockSpec((tm, d), lambda i, j: (i, 0)),
                  pl.BlockSpec((1, d), lambda i, j: (0, 0)),
                  pl.BlockSpec((None, 6, d), lambda i, j: (midx(i), 0, 0)),
                  pl.BlockSpec((None, d, tn), lambda i, j: (layer, 0, j)),
                  pl.BlockSpec((None, d, LANES), lambda i, j: (layer, 0, 0))],
        out_specs=[pl.BlockSpec((tm, tn), lambda i, j: (i, j)),
                   pl.BlockSpec((tm, LANES), lambda i, j: (i, 0))],
        out_shape=[jax.ShapeDtypeStruct((m, n), F32), jax.ShapeDtypeStruct((m, LANES), F32)],
        scratch_shapes=[pltpu.VMEM((tm, d), BF16)],
        compiler_params=_cparams("parallel", "arbitrary"),
        name="in_proj",
    )(x, nw, mod, w_main, w_ab)


def _shift_rows(x, d, row, t_len):
    if d == 0:
        return x
    y = pltpu.roll(x, (-d) % t_len, axis=0)
    ok = (row + d >= 0) & (row + d < t_len)
    return jnp.where(ok, y, 0.0)


def _conv3(x, w, row, t_len):
    return (_shift_rows(x, -1, row, t_len) * w[0:1, :] + x * w[1:2, :]
            + _shift_rows(x, 1, row, t_len) * w[2:3, :])


def _chunk_cumsum(g, row, t_len, reverse):
    pos = row % DELTA_CHUNK
    s = 1
    while s < DELTA_CHUNK:
        if reverse:
            g = g + jnp.where(pos < DELTA_CHUNK - s, pltpu.roll(g, t_len - s, axis=0), 0.0)
        else:
            g = g + jnp.where(pos >= s, pltpu.roll(g, s, axis=0), 0.0)
        s *= 2
    return g


def _unit_tri_inverse(lmat, r, c):
    eye = jnp.where(r == c, 1.0, 0.0)

    def same_block(b):
        return (r // b) == (c // b)

    x = jnp.where(same_block(8), -lmat, 0.0)
    x2 = _bmm16(x, x)
    x4 = _bmm16(x2, x2)
    p = _bmm16(_bmm16(eye + x, eye + x2), eye + x4)
    b = 8
    while b < DELTA_CHUNK:
        off = jnp.where(same_block(2 * b) & jnp.logical_not(same_block(b)), lmat, 0.0)
        pb = p.astype(BF16)
        p = p - _bmm16(_bmm16(pb, off), pb)
        b *= 2
    return p


def _delta_kernel(*refs, t_len, hps, has_s0, write_state):
    (q_ref, k_ref, v_ref, ga_ref, ab_ref, cq_ref, ck_ref, cv_ref, alog_ref, dtb_ref, nw_ref), rest = refs[:11], refs[11:]
    if has_s0:
        s0_ref, rest = rest[0], rest[1:]
    o_ref, rest = rest[0], rest[1:]
    if write_state:
        sout_ref = rest[0]

    n_ch = t_len // DELTA_CHUNK
    cl = DELTA_CHUNK
    row = lax.broadcasted_iota(jnp.int32, (t_len, LANES), 0)
    lane = lax.broadcasted_iota(jnp.int32, (t_len, LANES), 1)
    r64 = lax.broadcasted_iota(jnp.int32, (cl, cl), 0)
    c64 = lax.broadcasted_iota(jnp.int32, (cl, cl), 1)

    def l2n(x):
        return x * lax.rsqrt(jnp.sum(x * x, axis=-1, keepdims=True) + 1e-6)

    def column(a, idx):
        col = jnp.sum(jnp.where(lane == idx, a, 0.0), axis=1, keepdims=True)
        return jnp.broadcast_to(col, (t_len, LANES))

    def chunks(a):
        return a.reshape(n_ch, cl, LANES)

    ab = ab_ref[...]
    g_all = -jnp.exp(alog_ref[...]) * _softplus(ab + dtb_ref[...])
    b_all = _sigmoid(ab)

    chains = []
    for hh in range(hps):
        head = pl.program_id(1) * hps + hh
        cs = slice(hh * LANES, (hh + 1) * LANES)
        q = l2n(_silu(_conv3(q_ref[:, cs], cq_ref[:, cs], row, t_len))) * (LANES ** -0.5)
        k = l2n(_silu(_conv3(k_ref[:, cs], ck_ref[:, cs], row, t_len)))
        v = _silu(_conv3(v_ref[:, cs], cv_ref[:, cs], row, t_len))
        k3 = chunks(k)
        kb16 = k3.astype(BF16)
        kk = jnp.einsum('ncd,nsd->ncs', kb16, kb16, preferred_element_type=F32)
        qk = jnp.einsum('ncd,nsd->ncs', chunks(q).astype(BF16), kb16, preferred_element_type=F32)
        for direction in range(2):
            rev = direction == 1
            g = column(g_all, direction * H_A + head)
            beta = column(b_all, 2 * H_A + direction * H_A + head)
            gc = _chunk_cumsum(g, row, t_len, rev)
            gc3 = chunks(gc)
            gc_rows = jnp.swapaxes(gc3, 1, 2)[:, :cl, :]
            dmat = gc3[:, :, :cl] - gc_rows
            if rev:
                incl, strict = c64 >= r64, c64 > r64
            else:
                incl, strict = c64 <= r64, c64 < r64
            decay = jnp.exp(jnp.where(incl, dmat, 0.0))
            lmat = jnp.where(strict, chunks(beta)[:, :, :cl] * kk * decay, 0.0)
            tinv = _unit_tri_inverse(lmat, r64, c64).astype(BF16)
            egc = jnp.exp(gc)
            last = 0 if rev else cl - 1
            glast = gc3[:, last:last + 1, :]
            if has_s0:
                s_init = s0_ref[direction, hh]
            else:
                s_init = jnp.zeros((LANES, LANES), F32)
            uw = _bmm16(tinv, jnp.concatenate([chunks(v * beta), chunks(k * beta * egc)], axis=2))
            attn = jnp.where(incl, qk * decay, 0.0)
            kdt = jnp.swapaxes(k3 * jnp.exp(glast - gc3), 1, 2)
            chains.append(dict(
                hh=hh, direction=direction, s=s_init, outs=[None] * n_ch,
                order=list(range(n_ch - 1, -1, -1) if rev else range(n_ch)),
                u=uw[:, :, :LANES],
                w_qg=jnp.concatenate([uw[:, :, LANES:], chunks(q * egc)], axis=1).astype(BF16),
                attn_kdt=jnp.concatenate([attn, kdt], axis=1).astype(BF16),
                gl=jnp.exp(glast)))

    for t in range(n_ch):
        for ch in chains:
            n = ch['order'][t]
            ws_qs = _dot(ch['w_qg'][n], ch['s'].astype(BF16))
            v_new = ch['u'][n] - ws_qs[:cl]
            av_kv = _dot(ch['attn_kdt'][n], v_new.astype(BF16))
            ch['outs'][n] = ws_qs[cl:] + av_kv[:cl]
            ch['s'] = ch['s'] * ch['gl'][n] + av_kv[cl:]

    for ch in chains:
        if write_state:
            sout_ref[ch['direction'], ch['hh']] = ch['s']
    for hh in range(hps):
        cs = slice(hh * LANES, (hh + 1) * LANES)
        fwd, bwd = chains[2 * hh], chains[2 * hh + 1]
        o = jnp.concatenate([a + b for a, b in zip(fwd['outs'], bwd['outs'])], axis=0)
        o_ref[:, cs] = (_rms(o, nw_ref[...]) * _silu(ga_ref[:, cs])).astype(o_ref.dtype)


def _delta_call(proj, ab, conv_t, alog, dtb, nw, s0, layer, *, t_len, nseq, blk0, write_state, hps=2):
    has_s0 = s0 is not None
    w_a = H_A * LANES
    wh = hps * LANES
    nhb = H_A // hps

    def pspec(part):
        return pl.BlockSpec((t_len, wh), lambda b, h: (blk0 + b, part * nhb + h))

    def cspec(part):
        return pl.BlockSpec((3, wh), lambda b, h: (0, part * nhb + h))

    row1 = pl.BlockSpec((1, LANES), lambda b, h: (0, 0))
    in_specs = [pspec(0), pspec(1), pspec(2), pspec(3),
                pl.BlockSpec((t_len, LANES), lambda b, h: (blk0 + b, 0)),
                cspec(0), cspec(1), cspec(2), row1, row1, row1]
    args = [proj, proj, proj, proj, ab, conv_t, conv_t, conv_t, alog, dtb, nw]
    if has_s0:
        in_specs.append(pl.BlockSpec((None, None, 2, hps, LANES, LANES), lambda b, h: (b, layer, 0, h, 0, 0)))
        args.append(s0)
    out_specs = [pl.BlockSpec((t_len, wh), lambda b, h: (b, h))]
    out_shape = [jax.ShapeDtypeStruct((nseq * t_len, w_a), BF16)]
    if write_state:
        out_specs.append(pl.BlockSpec((None, 2, hps, LANES, LANES), lambda b, h: (b, 0, h, 0, 0)))
        out_shape.append(jax.ShapeDtypeStruct((nseq, 2, H_A, LANES, LANES), F32))

    return pl.pallas_call(
        functools.partial(_delta_kernel, t_len=t_len, hps=hps, has_s0=has_s0, write_state=write_state),
        grid=(nseq, nhb),
        in_specs=in_specs,
        out_specs=out_specs,
        out_shape=out_shape,
        compiler_params=_cparams("parallel", "parallel"),
        name=f"delta_T{t_len}",
    )(*args)


def _sgu_kernel(z_ref, nw_ref, w_ref, b_ref, o_ref, *, t_len):
    wb = o_ref.shape[1]
    z = _gelu_tanh(z_ref[...])
    u, v = z[:, :wb], z[:, wb:]
    mu = jnp.mean(v, axis=-1, keepdims=True)
    vc = v - mu
    vn = (vc * lax.rsqrt(jnp.mean(vc * vc, axis=-1, keepdims=True) + 1e-5) * nw_ref[...]).astype(BF16)
    for n in range(t_len // SGU_CHUNK):
        rs = slice(n * SGU_CHUNK, (n + 1) * SGU_CHUNK)
        for g in range(wb // LANES):
            cs = slice(g * LANES, (g + 1) * LANES)
            sp = _dot(w_ref[g], vn[rs, cs]) + b_ref[g]
            o_ref[rs, cs] = (u[rs, cs] * sp).astype(o_ref.dtype)


def _sconv_kernel(c_ref, w_ref, o_ref, *, t_len):
    wc = o_ref.shape[1]
    row = lax.broadcasted_iota(jnp.int32, (t_len, wc), 0)
    x = c_ref[...]
    y = _conv3(x[:, wc:2 * wc] * x[:, 2 * wc:], w_ref[...], row, t_len)
    o_ref[...] = (x[:, :wc] * y).astype(o_ref.dtype)


def _pool_kernel(p_ref, w_ref, sc_ref, o_ref, *, t_len):
    row = lax.broadcasted_iota(jnp.int32, (t_len, LANES), 0)
    for j, win in enumerate(POOL_WINDOWS):
        cs = slice(j * LANES, (j + 1) * LANES)
        x = p_ref[:, cs]
        half = win // 2
        acc = x
        for d in range(-half, win - half):
            if d != 0:
                acc = acc + _shift_rows(x, d, row, t_len)
        cnt = (jnp.minimum(row + (win - half), t_len) - jnp.maximum(row - half, 0)).astype(F32)
        pooled = acc / cnt - x
        o_ref[:, cs] = (_dot(pooled.astype(BF16), w_ref[j]) * sc_ref[:, cs]).astype(o_ref.dtype)


def _seq_call(kern, name, proj, col_block, width_in, params, pspecs, *, t_len, nseq, blk0, width_out):
    return pl.pallas_call(
        functools.partial(kern, t_len=t_len),
        grid=(nseq,),
        in_specs=[pl.BlockSpec((t_len, width_in), lambda b: (blk0 + b, col_block))] + pspecs,
        out_specs=pl.BlockSpec((t_len, width_out), lambda b: (b, 0)),
        out_shape=jax.ShapeDtypeStruct((nseq * t_len, width_out), BF16),
        compiler_params=_cparams("parallel"),
        name=f"{name}_T{t_len}",
    )(proj, *params)


def _outproj_kernel(x_ref, *refs, n_ctx_tiles):
    n_mix = (len(refs) - 3) // 2
    ctx_refs, lat_refs = refs[:n_mix], refs[n_mix:2 * n_mix]
    w_ref, mod_ref, o_ref = refs[2 * n_mix:]

    def run(mix_refs):
        acc = None
        for i, m_ref in enumerate(mix_refs):
            wq = m_ref.shape[1]
            part = _dot(m_ref[...], w_ref[i * wq:(i + 1) * wq, :])
            acc = part if acc is None else acc + part
        o_ref[...] = x_ref[...] + mod_ref[2:3, :] * acc

    is_ctx = pl.program_id(0) < n_ctx_tiles
    pl.when(is_ctx)(lambda: run(ctx_refs))
    pl.when(jnp.logical_not(is_ctx))(lambda: run(lat_refs))


def _outproj_call(x, mixes_ctx, mixes_lat, w_out, layer, mod, mc, sd, tm):
    m, d = x.shape
    wq = mixes_ctx[0].shape[1]
    nct = mc // tm
    midx = functools.partial(_mod_row_index, tm=tm, mc=mc, sd=sd)
    cspec = pl.BlockSpec((tm, wq), lambda i: (jnp.minimum(i, nct - 1), 0))
    lspec = pl.BlockSpec((tm, wq), lambda i: (jnp.maximum(i - nct, 0), 0))
    n_mix = len(mixes_ctx)
    return pl.pallas_call(
        functools.partial(_outproj_kernel, n_ctx_tiles=nct),
        grid=(m // tm,),
        in_specs=[pl.BlockSpec((tm, d), lambda i: (i, 0))] + [cspec] * n_mix + [lspec] * n_mix
                 + [pl.BlockSpec((None, d, d), lambda i: (layer, 0, 0)),
                    pl.BlockSpec((None, 6, d), lambda i: (midx(i), 0, 0))],
        out_specs=pl.BlockSpec((tm, d), lambda i: (i, 0)),
        out_shape=jax.ShapeDtypeStruct((m, d), F32),
        compiler_params=_cparams("parallel"),
        name="out_proj",
    )(x, *mixes_ctx, *mixes_lat, w_out, mod)


def _ffn_kernel(x_ref, nw_ref, mod_ref, wg_ref, wu_ref, wd_ref, o_ref, h_ref, acc_ref):
    j = pl.program_id(1)

    @pl.when(j == 0)
    def _():
        h = _rms(x_ref[...], nw_ref[...]) * (1.0 + mod_ref[4:5, :]) + mod_ref[3:4, :]
        h_ref[...] = h.astype(BF16)
        acc_ref[...] = jnp.zeros_like(acc_ref)

    h = h_ref[...]
    a = (_silu(_dot(h, wg_ref[...])) * _dot(h, wu_ref[...])).astype(BF16)
    acc_ref[...] += _dot(a, wd_ref[...])

    @pl.when(j == pl.num_programs(1) - 1)
    def _():
        o_ref[...] = x_ref[...] + mod_ref[5:6, :] * acc_ref[...]


def _ffn_call(x, nw, mod, wg, wu, wd, layer, mc, sd, tm, tf):
    m, d = x.shape
    f = wg.shape[2]
    midx = functools.partial(_mod_row_index, tm=tm, mc=mc, sd=sd)
    return pl.pallas_call(
        _ffn_kernel,
        grid=(m // tm, f // tf),
        in_specs=[pl.BlockSpec((tm, d), lambda i, j: (i, 0)),
                  pl.BlockSpec((1, d), lambda i, j: (0, 0)),
                  pl.BlockSpec((None, 6, d), lambda i, j: (midx(i), 0, 0)),
                  pl.BlockSpec((None, d, tf), lambda i, j: (layer, 0, j)),
                  pl.BlockSpec((None, d, tf), lambda i, j: (layer, 0, j)),
                  pl.BlockSpec((None, tf, d), lambda i, j: (layer, j, 0))],
        out_specs=pl.BlockSpec((tm, d), lambda i, j: (i, 0)),
        out_shape=jax.ShapeDtypeStruct((m, d), F32),
        scratch_shapes=[pltpu.VMEM((tm, d), BF16), pltpu.VMEM((tm, d), F32)],
        compiler_params=_cparams("parallel", "arbitrary"),
        name="ffn_dense",
    )(x, nw, mod, wg, wu, wd)


def _route_kernel(x_ref, nw_ref, mod_ref, rw_ref, h_ref, r_ref, *, n_experts):
    tm = x_ref.shape[0]
    h = _rms(x_ref[...], nw_ref[...]) * (1.0 + mod_ref[4:5, :]) + mod_ref[3:4, :]
    for cb in range(ROW_CHUNKS):
        h_ref[pl.ds(cb, tm, stride=ROW_CHUNKS), :] = h[:, cb * LANES:(cb + 1) * LANES]
    lane = lax.broadcasted_iota(jnp.int32, (tm, LANES), 1).astype(F32)
    neg = jnp.float32(-jnp.inf)
    logits = jnp.where(lane < n_experts, _dot3(h, rw_ref[...]), neg)
    m1 = jnp.max(logits, axis=-1, keepdims=True)
    i1 = jnp.min(jnp.where(logits == m1, lane, float(LANES)), axis=-1, keepdims=True)
    rest = jnp.where(lane == i1, neg, logits)
    m2 = jnp.max(rest, axis=-1, keepdims=True)
    i2 = jnp.min(jnp.where(rest == m2, lane, float(LANES)), axis=-1, keepdims=True)
    e2 = jnp.exp(m2 - m1)
    den = 1.0 + e2
    r_ref[...] = jnp.where(lane == 0.0, i1,
                           jnp.where(lane == 1.0, i2,
                                     jnp.where(lane == 2.0, 1.0 / den, jnp.where(lane == 3.0, e2 / den, 0.0))))


def _route_call(x, nw, mod, rw, mc, sd, tm, n_experts):
    m, d = x.shape
    midx = functools.partial(_mod_row_index, tm=tm, mc=mc, sd=sd)
    return pl.pallas_call(
        functools.partial(_route_kernel, n_experts=n_experts),
        grid=(m // tm,),
        in_specs=[pl.BlockSpec((tm, d), lambda i: (i, 0)),
                  pl.BlockSpec((1, d), lambda i: (0, 0)),
                  pl.BlockSpec((None, 6, d), lambda i: (midx(i), 0, 0)),
                  pl.BlockSpec((d, LANES), lambda i: (0, 0))],
        out_specs=[pl.BlockSpec((tm * ROW_CHUNKS, LANES), lambda i: (i, 0)),
                   pl.BlockSpec((tm, LANES), lambda i: (i, 0))],
        out_shape=[jax.ShapeDtypeStruct((m * ROW_CHUNKS, LANES), F32),
                   jax.ShapeDtypeStruct((m, LANES), F32)],
        compiler_params=_cparams("parallel"),
        name="moe_route",
    )(x, nw, mod, rw)


def _row_copy(src_hbm, dst_ref, src_row, dst_row, sem):
    s0 = pl.multiple_of(src_row * ROW_CHUNKS, ROW_CHUNKS)
    d0 = pl.multiple_of(dst_row * ROW_PITCH, 8)
    return pltpu.make_async_copy(src_hbm.at[pl.ds(s0, ROW_CHUNKS), :], dst_ref.at[pl.ds(d0, ROW_CHUNKS), :], sem)


def _gather_kernel(nv_ref, idx_ref, src_hbm, o_ref, buf_ref, sem, *, rows):
    blk = pl.program_id(0)
    n_valid = nv_ref[blk]

    @pl.when(blk == 0)
    def _():
        buf_ref[...] = jnp.zeros_like(buf_ref)

    def start(r, carry):
        _row_copy(src_hbm, buf_ref, idx_ref[0, r], r, sem).start()
        return carry

    def wait(r, carry):
        _row_copy(src_hbm, buf_ref, idx_ref[0, r], r, sem).wait()
        return carry

    lax.fori_loop(0, n_valid, start, 0)
    lax.fori_loop(0, n_valid, wait, 0)
    keep = lax.broadcasted_iota(jnp.int32, (rows, LANES), 0) < n_valid
    for cb in range(ROW_CHUNKS):
        val = buf_ref[pl.ds(cb, rows, stride=ROW_PITCH), :]
        o_ref[:, cb * LANES:(cb + 1) * LANES] = jnp.where(keep, val, 0.0).astype(o_ref.dtype)


def _gather_call(h_rows, src_idx, n_valid, rows):
    p = src_idx.shape[0]
    d = ROW_CHUNKS * LANES
    return pl.pallas_call(
        functools.partial(_gather_kernel, rows=rows),
        grid_spec=pltpu.PrefetchScalarGridSpec(
            num_scalar_prefetch=1,
            grid=(p // rows,),
            in_specs=[pl.BlockSpec((None, 1, rows), lambda i, n: (i, 0, 0), memory_space=pltpu.SMEM),
                      pl.BlockSpec(memory_space=pl.ANY)],
            out_specs=pl.BlockSpec((rows, d), lambda i, n: (i, 0)),
            scratch_shapes=[pltpu.VMEM((rows * ROW_PITCH, LANES), F32), pltpu.SemaphoreType.DMA(())]),
        out_shape=jax.ShapeDtypeStruct((p, d), BF16),
        compiler_params=_cparams("arbitrary"),
        name="moe_gather",
    )(n_valid, src_idx.reshape(p // rows, 1, rows), h_rows)


def _experts_kernel(te_ref, nt_ref, ns_ref, x_ref, wg_ref, wu_ref, wd_ref, o_ref, acc_ref, *, sub):
    i, j = pl.program_id(0), pl.program_id(1)
    tm = x_ref.shape[0]
    last_j = pl.num_programs(1) - 1
    used = i < nt_ref[0]

    for k in range(1, tm // sub + 1):
        @pl.when(used & (ns_ref[i] == k))
        def _():
            rows = k * sub
            x = x_ref[:rows, :]
            a = (_silu(_dot(x, wg_ref[...].astype(BF16))) * _dot(x, wu_ref[...].astype(BF16))).astype(BF16)
            part = _dot(a, wd_ref[...].astype(BF16))

            @pl.when(j == 0)
            def _():
                acc_ref[:rows, :] = part
                if rows < tm:
                    acc_ref[rows:, :] = jnp.zeros((tm - rows, acc_ref.shape[1]), F32)

            @pl.when(j != 0)
            def _():
                acc_ref[:rows, :] += part

    @pl.when(used & (j == last_j))
    def _():
        for cb in range(ROW_CHUNKS):
            o_ref[pl.ds(cb, tm, stride=ROW_CHUNKS), :] = acc_ref[:, cb * LANES:(cb + 1) * LANES]

    @pl.when(jnp.logical_not(used) & (j == last_j))
    def _():
        o_ref[...] = jnp.zeros_like(o_ref)


def _experts_call(tile_e, ntiles, nsub, xs, wg, wu, wd, layer, tm, tf, sub):
    d, f = wg.shape[2:]
    p = xs.shape[0]
    nf = f // tf

    def row_blk(i, j, te, nt, ns):
        return (jnp.minimum(i, nt[0] - 1), 0)

    def jj(i, j, nt):
        return jnp.where(i < nt[0], j, nf - 1)

    def w_in_blk(i, j, te, nt, ns):
        return (layer, te[i], 0, jj(i, j, nt))

    def w_out_blk(i, j, te, nt, ns):
        return (layer, te[i], jj(i, j, nt), 0)

    return pl.pallas_call(
        functools.partial(_experts_kernel, sub=sub),
        grid_spec=pltpu.PrefetchScalarGridSpec(
            num_scalar_prefetch=3,
            grid=(p // tm, nf),
            in_specs=[pl.BlockSpec((tm, d), row_blk),
                      pl.BlockSpec((None, None, d, tf), w_in_blk),
                      pl.BlockSpec((None, None, d, tf), w_in_blk),
                      pl.BlockSpec((None, None, tf, d), w_out_blk)],
            out_specs=pl.BlockSpec((tm * ROW_CHUNKS, LANES), lambda i, j, te, nt, ns: (i, 0)),
            scratch_shapes=[pltpu.VMEM((tm, d), F32)]),
        out_shape=jax.ShapeDtypeStruct((p * ROW_CHUNKS, LANES), F32),
        compiler_params=_cparams("arbitrary", "arbitrary"),
        name="moe_experts",
    )(tile_e, ntiles, nsub, xs, wg, wu, wd)


def _combine_kernel(idx_ref, x_ref, r_ref, mod_ref, ys_hbm, o_ref, buf_ref, sem):
    tc = x_ref.shape[0]
    n_rows = TOP_K * tc

    def start(r, carry):
        _row_copy(ys_hbm, buf_ref, idx_ref[0, r], r, sem).start()
        return carry

    def wait(r, carry):
        _row_copy(ys_hbm, buf_ref, idx_ref[0, r], r, sem).wait()
        return carry

    lax.fori_loop(0, n_rows, start, 0, unroll=8)
    lax.fori_loop(0, n_rows, wait, 0, unroll=8)
    w1 = r_ref[:, 2:3]
    w2 = r_ref[:, 3:4]
    for cb in range(ROW_CHUNKS):
        cs = slice(cb * LANES, (cb + 1) * LANES)
        y1 = buf_ref[pl.ds(cb, tc, stride=ROW_PITCH), :]
        y2 = buf_ref[pl.ds(tc * ROW_PITCH + cb, tc, stride=ROW_PITCH), :]
        o_ref[:, cs] = x_ref[:, cs] + mod_ref[5:6, cs] * (y1 * w1 + y2 * w2)


def _combine_call(dest, x, route, mod, ys, mc, sd, tc):
    m, d = x.shape
    midx = functools.partial(_mod_row_index, tm=tc, mc=mc, sd=sd)
    return pl.pallas_call(
        _combine_kernel,
        grid=(m // tc,),
        in_specs=[pl.BlockSpec((None, 1, TOP_K * tc), lambda i: (i, 0, 0), memory_space=pltpu.SMEM),
                  pl.BlockSpec((tc, d), lambda i: (i, 0)),
                  pl.BlockSpec((tc, LANES), lambda i: (i, 0)),
                  pl.BlockSpec((None, 6, d), lambda i: (midx(i), 0, 0)),
                  pl.BlockSpec(memory_space=pl.ANY)],
        out_specs=pl.BlockSpec((tc, d), lambda i: (i, 0)),
        out_shape=jax.ShapeDtypeStruct((m, d), F32),
        scratch_shapes=[pltpu.VMEM((TOP_K * tc * ROW_PITCH, LANES), F32), pltpu.SemaphoreType.DMA(())],
        compiler_params=_cparams("arbitrary"),
        name="moe_combine",
    )(dest, x, route, mod, ys)


def _moe_plan(route, n_experts, tm, sub, n_tiles, tc, gather_rows):
    m = route.shape[0]
    e_flat = jnp.concatenate([route[:, 0], route[:, 1]]).astype(jnp.int32)
    onehot = (e_flat[:, None] == jnp.arange(n_experts, dtype=jnp.int32)[None, :]).astype(jnp.int32)
    csum = jnp.cumsum(onehot, axis=0)
    cnt = csum[-1]
    rank = jnp.take_along_axis(csum, e_flat[:, None], axis=1)[:, 0] - 1
    gsz = ((cnt + tm - 1) // tm) * tm
    off_end = jnp.cumsum(gsz)
    off = off_end - gsz
    dest = off[e_flat] + rank
    n_used = off_end[-1] // tm

    def owner(starts):
        return jnp.minimum(jnp.sum((off_end[None, :] <= starts[:, None]).astype(jnp.int32), axis=1), n_experts - 1)

    tiles = jnp.arange(n_tiles, dtype=jnp.int32)
    tile_e = owner(tiles * tm)
    rows_left = (off + cnt)[tile_e] - tiles * tm
    n_sub = jnp.clip((rows_left + sub - 1) // sub, 1, tm // sub)
    tile_e = jnp.where(tiles < n_used, tile_e, tile_e[jnp.maximum(n_used - 1, 0)])
    blocks = jnp.arange(n_tiles * tm // gather_rows, dtype=jnp.int32) * gather_rows
    n_valid = jnp.where(blocks < off_end[-1], jnp.clip((off + cnt)[owner(blocks)] - blocks, 0, gather_rows), 0)
    tok = jnp.tile(jnp.arange(m, dtype=jnp.int32), TOP_K)
    src = jnp.zeros((n_tiles * tm,), jnp.int32).at[dest].set(tok)
    dest_tiles = jnp.concatenate([dest[:m].reshape(m // tc, 1, tc), dest[m:].reshape(m // tc, 1, tc)], axis=2)
    i32 = lambda a: a.astype(jnp.int32)
    return src, dest_tiles, i32(tile_e), i32(n_used.reshape(1)), i32(n_sub), i32(n_valid)


def _moe_layer(x, nw, mod, rw, wg, wu, wd, layer, mc, sd, tm_route, tm, sub, tf, tc, gather_rows):
    m, d = x.shape
    n_experts = wg.shape[1]
    assert tm % gather_rows == 0 and tm % sub == 0
    n_tiles = -(-(TOP_K * m) // tm) + n_experts
    h_rows, route = _route_call(x, nw, mod, rw, mc, sd, tm_route, n_experts)
    src, dest_tiles, tile_e, n_used, n_sub, n_valid = _moe_plan(route, n_experts, tm, sub, n_tiles, tc, gather_rows)
    xs = _gather_call(h_rows, src, n_valid, gather_rows)
    ys = _experts_call(tile_e, n_used, n_sub, xs, wg, wu, wd, layer, tm, tf, sub)
    return _combine_call(dest_tiles, x, route, mod, ys, mc, sd, tc)


def _final_kernel(x_ref, w_ref, o_ref):
    o_ref[...] = _rms(x_ref[...], w_ref[...])


def _final_call(x, w, row0, rows, tm):
    d = x.shape[1]
    blk0 = row0 // tm
    return pl.pallas_call(
        _final_kernel,
        grid=(rows // tm,),
        in_specs=[pl.BlockSpec((tm, d), lambda i: (blk0 + i, 0)), pl.BlockSpec((1, d), lambda i: (0, 0))],
        out_specs=pl.BlockSpec((tm, d), lambda i: (i, 0)),
        out_shape=jax.ShapeDtypeStruct((rows, d), F32),
        compiler_params=_cparams("parallel"),
        name="final_norm",
    )(x, w)


def _sincos_2d(rows, cols, dim):
    quarter = dim // 4
    omega = 1.0 / (10000.0 ** (jnp.arange(quarter, dtype=F32) / quarter))
    r = jnp.arange(rows, dtype=F32)[:, None] * omega
    cc = jnp.arange(cols, dtype=F32)[:, None] * omega
    r_emb = jnp.concatenate([jnp.sin(r), jnp.cos(r)], axis=-1)
    c_emb = jnp.concatenate([jnp.sin(cc), jnp.cos(cc)], axis=-1)
    emb = jnp.concatenate([jnp.broadcast_to(r_emb[:, None, :], (rows, cols, dim // 2)),
                           jnp.broadcast_to(c_emb[None, :, :], (rows, cols, dim // 2))], axis=-1)
    return emb.reshape(rows * cols, dim)


def _pad_lanes(a):
    return jnp.pad(a, ((0, 0), (0, LANES - a.shape[1])))


def kernel(x_prompt, x_sample, c, state_delta, c_ctx, norm1_w, norm2_w, w_mod, b_mod, w_in, w_out, qkv_conv_w, delta_a_log, delta_dt_bias, delta_norm_w, sgu_norm_w, sgu_w, sgu_b, sconv_w, pool_w, pool_scale, ffn_w_gate, ffn_w_up, ffn_w_down, router_w, moe_w_gate, moe_w_up, moe_w_down, final_norm_w):
    bc, sc, d = x_prompt.shape
    bd, sd, _ = x_sample.shape
    depth = w_in.shape[0]
    mc, ml = bc * sc, bd * sd
    m = mc + ml
    w_a = H_A * LANES
    w_b, w_c, w_d = sgu_norm_w.shape[1], sconv_w.shape[1], pool_scale.shape[1]
    assert d == ROW_CHUNKS * LANES and mc % sd == 0 and w_a == w_b == w_c == w_d == d // 4
    assert delta_a_log.shape[1:] == (2, H_A) and delta_norm_w.shape[1] == LANES

    tm_mm = 512 if m % 512 == 0 else 256
    grid_w = 64
    pos = _sincos_2d(sd // grid_w, grid_w, d)
    x = jnp.concatenate([x_prompt.reshape(mc, d), (x_sample + pos[None]).reshape(ml, d)], axis=0)

    rm = -(-(1 + bd) // 8) * 8
    cond = jnp.zeros((rm, d), F32).at[0].set(c_ctx).at[1:1 + bd].set(c)
    mod_all = _mod_call(cond, w_mod, b_mod).reshape(depth, rm, 6, d)

    o0 = 3 * w_a
    o1 = o0 + w_a
    o3 = o1 + 4 * H_A
    o4 = o3 + 2 * w_b
    o5 = o4 + 3 * w_c
    blk0 = mc // sd
    w_main_all = jnp.concatenate([w_in[:, :, :o1], w_in[:, :, o3:]], axis=2).astype(BF16)
    w_ab_all = jnp.pad(w_in[:, :, o1:o3], ((0, 0), (0, 0), (0, LANES - (o3 - o1)))).astype(BF16)
    w_out_b = w_out.astype(BF16)
    ffn_b = [w.astype(BF16) for w in (ffn_w_gate, ffn_w_up, ffn_w_down)]
    moe_w = (moe_w_gate, moe_w_up, moe_w_down)
    ctx_states = []
    for l in range(depth):
        mod = mod_all[l]
        tm_in = 1024 if (mc % 1024 == 0 and sd % 1024 == 0) else tm_mm
        proj, ab = _inproj_call(x, norm1_w[l][None], mod, w_main_all, w_ab_all, l, mc, sd, tm_in, 1024)

        conv_t = qkv_conv_w[l].T
        alog = _pad_lanes(delta_a_log[l].reshape(1, 2 * H_A))
        dtb = _pad_lanes(delta_dt_bias[l].reshape(1, 2 * H_A))
        nwa = delta_norm_w[l][None]
        a_ctx, s_ctx = _delta_call(proj, ab, conv_t, alog, dtb, nwa, None, l,
                                   t_len=sc, nseq=bc, blk0=0, write_state=True)
        (a_lat,) = _delta_call(proj, ab, conv_t, alog, dtb, nwa, state_delta, l,
                               t_len=sd, nseq=bd, blk0=blk0, write_state=False)
        ctx_states.append(s_ctx)

        def both_paths(kern, name, col_block, width_in, params, pspecs, width_out):
            return (_seq_call(kern, name, proj, col_block, width_in, params, pspecs,
                              t_len=sc, nseq=bc, blk0=0, width_out=width_out),
                    _seq_call(kern, name, proj, col_block, width_in, params, pspecs,
                              t_len=sd, nseq=bd, blk0=blk0, width_out=width_out))

        n_g = sgu_w.shape[1]
        b_ctx, b_lat = both_paths(
            _sgu_kernel, "sgu", o1 // (2 * w_b), 2 * w_b,
            [sgu_norm_w[l][None], sgu_w[l].astype(BF16), sgu_b[l].reshape(n_g, SGU_CHUNK, 1)],
            [pl.BlockSpec((1, w_b), lambda b: (0, 0)),
             pl.BlockSpec((n_g, SGU_CHUNK, SGU_CHUNK), lambda b: (0, 0, 0)),
             pl.BlockSpec((n_g, SGU_CHUNK, 1), lambda b: (0, 0, 0))], w_b)
        c_ctx_mix, c_lat = both_paths(
            _sconv_kernel, "sconv", (o1 + 2 * w_b) // (3 * w_c), 3 * w_c,
            [sconv_w[l].T], [pl.BlockSpec((3, w_c), lambda b: (0, 0))], w_c)
        n_gd = pool_w.shape[1]
        d_ctx, d_lat = both_paths(
            _pool_kernel, "pool", (o1 + 2 * w_b + 3 * w_c) // w_d, w_d,
            [pool_w[l].astype(BF16), pool_scale[l][None]],
            [pl.BlockSpec((n_gd, LANES, LANES), lambda b: (0, 0, 0)),
             pl.BlockSpec((1, w_d), lambda b: (0, 0))], w_d)

        x = _outproj_call(x, (a_ctx, b_ctx, c_ctx_mix, d_ctx), (a_lat, b_lat, c_lat, d_lat),
                          w_out_b, l, mod, mc, sd, tm_mm)

        jl = l // 2
        if l % 2 == 0:
            x = _ffn_call(x, norm2_w[l][None], mod, *ffn_b, jl, mc, sd, tm_mm, 512)
        else:
            x = _moe_layer(x, norm2_w[l][None], mod, _pad_lanes(router_w[jl]), *moe_w, jl,
                           mc, sd, tm_route=256, tm=640, sub=128, tf=512, tc=256, gather_rows=320)

    y_prompt = _final_call(x, final_norm_w[None], 0, mc, tm_mm).reshape(bc, sc, d)
    y_sample = _final_call(x, final_norm_w[None], mc, ml, tm_mm).reshape(bd, sd, d)
    new_state = jnp.stack(ctx_states, axis=1).astype(x_prompt.dtype)
    return (y_prompt, y_sample, new_state)
```

```python
import functools

import jax
import jax.numpy as jnp
from jax import lax
from jax.experimental import pallas as pl
from jax.experimental.pallas import tpu as pltpu

F32 = jnp.float32
BF16 = jnp.bfloat16

LANES = 128
ROW_CHUNKS = 16
ROW_PITCH = 24
DELTA_CHUNK = 64
SGU_CHUNK = 128
H_A = 4
POOL_WINDOWS = (2, 4, 8, 16)
TOP_K = 2
VMEM_LIMIT = 56 * 1024 * 1024


def _cparams(*sem):
    return pltpu.CompilerParams(dimension_semantics=sem, vmem_limit_bytes=VMEM_LIMIT)


def _silu(x):
    return x / (1.0 + jnp.exp(-x))


def _sigmoid(x):
    return 1.0 / (1.0 + jnp.exp(-x))


def _softplus(x):
    return jnp.maximum(x, 0.0) + jnp.log1p(jnp.exp(-jnp.abs(x)))


def _gelu_tanh(x):
    return 0.5 * x * (1.0 + jnp.tanh(0.7978845608028654 * (x + 0.044715 * (x * x * x))))


def _rms(x, w, eps=1e-6):
    return x * lax.rsqrt(jnp.mean(x * x, axis=-1, keepdims=True) + eps) * w


def _dot(a, b):
    return jnp.dot(a, b, preferred_element_type=F32)


def _split(a):
    hi = a.astype(BF16)
    lo = (a - hi.astype(F32)).astype(BF16)
    return hi, lo


def _dot3(a, b):
    a_hi, a_lo = _split(a)
    b_hi, b_lo = _split(b)
    return _dot(a_hi, b_hi) + _dot(a_hi, b_lo) + _dot(a_lo, b_hi)


def _bmm(a, b):
    return jnp.einsum('nij,njk->nik', a, b, preferred_element_type=F32)


def _bmm16(a, b):
    return _bmm(a.astype(BF16), b.astype(BF16))


def _mod_kernel(c_ref, w_ref, b_ref, o_ref):
    a = _silu(c_ref[...]).astype(BF16)
    o_ref[...] = _dot(a, w_ref[...].astype(BF16)) + b_ref[...]


def _mod_call(cond, w_mod, b_mod, tn=1024):
    depth, d, n = w_mod.shape
    rm = cond.shape[0]
    return pl.pallas_call(
        _mod_kernel,
        grid=(depth, n // tn),
        in_specs=[pl.BlockSpec((rm, d), lambda l, j: (0, 0)),
                  pl.BlockSpec((None, d, tn), lambda l, j: (l, 0, j)),
                  pl.BlockSpec((None, 1, tn), lambda l, j: (l, 0, j))],
        out_specs=pl.BlockSpec((None, rm, tn), lambda l, j: (l, 0, j)),
        out_shape=jax.ShapeDtypeStruct((depth, rm, n), F32),
        compiler_params=_cparams("parallel", "parallel"),
        name="adaln_mod",
    )(cond, w_mod, b_mod.reshape(depth, 1, n))


def _mod_row_index(i, tm, mc, sd):
    r0 = i * tm
    return jnp.where(r0 < mc, 0, 1 + (r0 - mc) // sd)


def _inproj_kernel(x_ref, nw_ref, mod_ref, w_ref, wab_ref, proj_ref, ab_ref, h_ref):
    @pl.when(pl.program_id(1) == 0)
    def _():
        h = _rms(x_ref[...], nw_ref[...]) * (1.0 + mod_ref[1:2, :]) + mod_ref[0:1, :]
        hb = h.astype(BF16)
        h_ref[...] = hb
        ab_ref[...] = _dot(hb, wab_ref[...])

    proj_ref[...] = _dot(h_ref[...], w_ref[...])


def _inproj_call(x, nw, mod, w_main, w_ab, layer, mc, sd, tm, tn):
    m, d = x.shape
    n = w_main.shape[2]
    midx = functools.partial(_mod_row_index, tm=tm, mc=mc, sd=sd)
    return pl.pallas_call(
        _inproj_kernel,
        grid=(m // tm, n // tn),
        in_specs=[pl.BlockSpec((tm, d), lambda i, j: (i, 0)),
                  pl.BlockSpec((1, d), lambda i, j: (0, 0)),
                  pl.BlockSpec((None, 6, d), lambda i, j: (midx(i), 0, 0)),
                  pl.BlockSpec((None, d, tn), lambda i, j: (layer, 0, j)),
                  pl.BlockSpec((None, d, LANES), lambda i, j: (layer, 0, 0))],
        out_specs=[pl.BlockSpec((tm, tn), lambda i, j: (i, j)),
                   pl.BlockSpec((tm, LANES), lambda i, j: (i, 0))],
        out_shape=[jax.ShapeDtypeStruct((m, n), F32), jax.ShapeDtypeStruct((m, LANES), F32)],
        scratch_shapes=[pltpu.VMEM((tm, d), BF16)],
        compiler_params=_cparams("parallel", "arbitrary"),
        name="in_proj",
    )(x, nw, mod, w_main, w_ab)


def _shift_rows(x, d, row, t_len):
    if d == 0:
        return x
    y = pltpu.roll(x, (-d) % t_len, axis=0)
    ok = (row + d >= 0) & (row + d < t_len)
    return jnp.where(ok, y, 0.0)


def _conv3(x, w, row, t_len):
    return (_shift_rows(x, -1, row, t_len) * w[0:1, :] + x * w[1:2, :]
            + _shift_rows(x, 1, row, t_len) * w[2:3, :])


def _chunk_cumsum(g, row, t_len, reverse):
    pos = row % DELTA_CHUNK
    s = 1
    while s < DELTA_CHUNK:
        if reverse:
            g = g + jnp.where(pos < DELTA_CHUNK - s, pltpu.roll(g, t_len - s, axis=0), 0.0)
        else:
            g = g + jnp.where(pos >= s, pltpu.roll(g, s, axis=0), 0.0)
        s *= 2
    return g


def _unit_tri_inverse(lmat, r, c):
    eye = jnp.where(r == c, 1.0, 0.0)

    def same_block(b):
        return (r // b) == (c // b)

    x = jnp.where(same_block(8), -lmat, 0.0)
    x2 = _bmm16(x, x)
    x4 = _bmm16(x2, x2)
    p = _bmm16(_bmm16(eye + x, eye + x2), eye + x4)
    b = 8
    while b < DELTA_CHUNK:
        off = jnp.where(same_block(2 * b) & jnp.logical_not(same_block(b)), lmat, 0.0)
        pb = p.astype(BF16)
        p = p - _bmm16(_bmm16(pb, off), pb)
        b *= 2
    return p


def _delta_kernel(*refs, t_len, hps, has_s0, write_state):
    (q_ref, k_ref, v_ref, ga_ref, ab_ref, cq_ref, ck_ref, cv_ref, alog_ref, dtb_ref, nw_ref), rest = refs[:11], refs[11:]
    if has_s0:
        s0_ref, rest = rest[0], rest[1:]
    o_ref, rest = rest[0], rest[1:]
    if write_state:
        sout_ref = rest[0]

    n_ch = t_len // DELTA_CHUNK
    cl = DELTA_CHUNK
    row = lax.broadcasted_iota(jnp.int32, (t_len, LANES), 0)
    lane = lax.broadcasted_iota(jnp.int32, (t_len, LANES), 1)
    r64 = lax.broadcasted_iota(jnp.int32, (cl, cl), 0)
    c64 = lax.broadcasted_iota(jnp.int32, (cl, cl), 1)

    def l2n(x):
        return x * lax.rsqrt(jnp.sum(x * x, axis=-1, keepdims=True) + 1e-6)

    def column(a, idx):
        col = jnp.sum(jnp.where(lane == idx, a, 0.0), axis=1, keepdims=True)
        return jnp.broadcast_to(col, (t_len, LANES))

    def chunks(a):
        return a.reshape(n_ch, cl, LANES)

    ab = ab_ref[...]
    g_all = -jnp.exp(alog_ref[...]) * _softplus(ab + dtb_ref[...])
    b_all = _sigmoid(ab)

    chains = []
    for hh in range(hps):
        head = pl.program_id(1) * hps + hh
        cs = slice(hh * LANES, (hh + 1) * LANES)
        q = l2n(_silu(_conv3(q_ref[:, cs], cq_ref[:, cs], row, t_len))) * (LANES ** -0.5)
        k = l2n(_silu(_conv3(k_ref[:, cs], ck_ref[:, cs], row, t_len)))
        v = _silu(_conv3(v_ref[:, cs], cv_ref[:, cs], row, t_len))
        k3 = chunks(k)
        kb16 = k3.astype(BF16)
        kk = jnp.einsum('ncd,nsd->ncs', kb16, kb16, preferred_element_type=F32)
        qk = jnp.einsum('ncd,nsd->ncs', chunks(q).astype(BF16), kb16, preferred_element_type=F32)
        for direction in range(2):
            rev = direction == 1
            g = column(g_all, direction * H_A + head)
            beta = column(b_all, 2 * H_A + direction * H_A + head)
            gc = _chunk_cumsum(g, row, t_len, rev)
            gc3 = chunks(gc)
            gc_rows = jnp.swapaxes(gc3, 1, 2)[:, :cl, :]
            dmat = gc3[:, :, :cl] - gc_rows
            if rev:
                incl, strict = c64 >= r64, c64 > r64
            else:
                incl, strict = c64 <= r64, c64 < r64
            decay = jnp.exp(jnp.where(incl, dmat, 0.0))
            lmat = jnp.where(strict, chunks(beta)[:, :, :cl] * kk * decay, 0.0)
            tinv = _unit_tri_inverse(lmat, r64, c64).astype(BF16)
            egc = jnp.exp(gc)
            last = 0 if rev else cl - 1
            glast = gc3[:, last:last + 1, :]
            if has_s0:
                s_init = s0_ref[direction, hh]
            else:
                s_init = jnp.zeros((LANES, LANES), F32)
            uw = _bmm16(tinv, jnp.concatenate([chunks(v * beta), chunks(k * beta * egc)], axis=2))
            attn = jnp.where(incl, qk * decay, 0.0)
            kdt = jnp.swapaxes(k3 * jnp.exp(glast - gc3), 1, 2)
            chains.append(dict(
                hh=hh, direction=direction, s=s_init, outs=[None] * n_ch,
                order=list(range(n_ch - 1, -1, -1) if rev else range(n_ch)),
                u=uw[:, :, :LANES],
                w_qg=jnp.concatenate([uw[:, :, LANES:], chunks(q * egc)], axis=1).astype(BF16),
                attn_kdt=jnp.concatenate([attn, kdt], axis=1).astype(BF16),
                gl=jnp.exp(glast)))

    for t in range(n_ch):
        for ch in chains:
            n = ch['order'][t]
            ws_qs = _dot(ch['w_qg'][n], ch['s'].astype(BF16))
            v_new = ch['u'][n] - ws_qs[:cl]
            av_kv = _dot(ch['attn_kdt'][n], v_new.astype(BF16))
            ch['outs'][n] = ws_qs[cl:] + av_kv[:cl]
            ch['s'] = ch['s'] * ch['gl'][n] + av_kv[cl:]

    for ch in chains:
        if write_state:
            sout_ref[ch['direction'], ch['hh']] = ch['s']
    for hh in range(hps):
        cs = slice(hh * LANES, (hh + 1) * LANES)
        fwd, bwd = chains[2 * hh], chains[2 * hh + 1]
        o = jnp.concatenate([a + b for a, b in zip(fwd['outs'], bwd['outs'])], axis=0)
        o_ref[:, cs] = (_rms(o, nw_ref[...]) * _silu(ga_ref[:, cs])).astype(o_ref.dtype)


def _delta_call(proj, ab, conv_t, alog, dtb, nw, s0, layer, *, t_len, nseq, blk0, write_state, hps=2):
    has_s0 = s0 is not None
    w_a = H_A * LANES
    wh = hps * LANES
    nhb = H_A // hps

    def pspec(part):
        return pl.BlockSpec((t_len, wh), lambda b, h: (blk0 + b, part * nhb + h))

    def cspec(part):
        return pl.BlockSpec((3, wh), lambda b, h: (0, part * nhb + h))

    row1 = pl.BlockSpec((1, LANES), lambda b, h: (0, 0))
    in_specs = [pspec(0), pspec(1), pspec(2), pspec(3),
                pl.BlockSpec((t_len, LANES), lambda b, h: (blk0 + b, 0)),
                cspec(0), cspec(1), cspec(2), row1, row1, row1]
    args = [proj, proj, proj, proj, ab, conv_t, conv_t, conv_t, alog, dtb, nw]
    if has_s0:
        in_specs.append(pl.BlockSpec((None, None, 2, hps, LANES, LANES), lambda b, h: (b, layer, 0, h, 0, 0)))
        args.append(s0)
    out_specs = [pl.BlockSpec((t_len, wh), lambda b, h: (b, h))]
    out_shape = [jax.ShapeDtypeStruct((nseq * t_len, w_a), BF16)]
    if write_state:
        out_specs.append(pl.BlockSpec((None, 2, hps, LANES, LANES), lambda b, h: (b, 0, h, 0, 0)))
        out_shape.append(jax.ShapeDtypeStruct((nseq, 2, H_A, LANES, LANES), F32))

    return pl.pallas_call(
        functools.partial(_delta_kernel, t_len=t_len, hps=hps, has_s0=has_s0, write_state=write_state),
        grid=(nseq, nhb),
        in_specs=in_specs,
        out_specs=out_specs,
        out_shape=out_shape,
        compiler_params=_cparams("parallel", "parallel"),
        name=f"delta_T{t_len}",
    )(*args)


def _sgu_kernel(z_ref, nw_ref, w_ref, b_ref, o_ref, *, t_len):
    wb = o_ref.shape[1]
    z = _gelu_tanh(z_ref[...])
    u, v = z[:, :wb], z[:, wb:]
    mu = jnp.mean(v, axis=-1, keepdims=True)
    vc = v - mu
    vn = (vc * lax.rsqrt(jnp.mean(vc * vc, axis=-1, keepdims=True) + 1e-5) * nw_ref[...]).astype(BF16)
    for n in range(t_len // SGU_CHUNK):
        rs = slice(n * SGU_CHUNK, (n + 1) * SGU_CHUNK)
        for g in range(wb // LANES):
            cs = slice(g * LANES, (g + 1) * LANES)
            sp = _dot(w_ref[g], vn[rs, cs]) + b_ref[g]
            o_ref[rs, cs] = (u[rs, cs] * sp).astype(o_ref.dtype)


def _sconv_kernel(c_ref, w_ref, o_ref, *, t_len):
    wc = o_ref.shape[1]
    row = lax.broadcasted_iota(jnp.int32, (t_len, wc), 0)
    x = c_ref[...]
    y = _conv3(x[:, wc:2 * wc] * x[:, 2 * wc:], w_ref[...], row, t_len)
    o_ref[...] = (x[:, :wc] * y).astype(o_ref.dtype)


def _pool_kernel(p_ref, w_ref, sc_ref, o_ref, *, t_len):
    row = lax.broadcasted_iota(jnp.int32, (t_len, LANES), 0)
    for j, win in enumerate(POOL_WINDOWS):
        cs = slice(j * LANES, (j + 1) * LANES)
        x = p_ref[:, cs]
        half = win // 2
        acc = x
        for d in range(-half, win - half):
            if d != 0:
                acc = acc + _shift_rows(x, d, row, t_len)
        cnt = (jnp.minimum(row + (win - half), t_len) - jnp.maximum(row - half, 0)).astype(F32)
        pooled = acc / cnt - x
        o_ref[:, cs] = (_dot(pooled.astype(BF16), w_ref[j]) * sc_ref[:, cs]).astype(o_ref.dtype)


def _seq_call(kern, name, proj, col_block, width_in, params, pspecs, *, t_len, nseq, blk0, width_out):
    return pl.pallas_call(
        functools.partial(kern, t_len=t_len),
        grid=(nseq,),
        in_specs=[pl.BlockSpec((t_len, width_in), lambda b: (blk0 + b, col_block))] + pspecs,
        out_specs=pl.BlockSpec((t_len, width_out), lambda b: (b, 0)),
        out_shape=jax.ShapeDtypeStruct((nseq * t_len, width_out), BF16),
        compiler_params=_cparams("parallel"),
        name=f"{name}_T{t_len}",
    )(proj, *params)


def _outproj_kernel(x_ref, *refs, n_ctx_tiles):
    n_mix = (len(refs) - 3) // 2
    ctx_refs, lat_refs = refs[:n_mix], refs[n_mix:2 * n_mix]
    w_ref, mod_ref, o_ref = refs[2 * n_mix:]

    def run(mix_refs):
        acc = None
        for i, m_ref in enumerate(mix_refs):
            wq = m_ref.shape[1]
            part = _dot(m_ref[...], w_ref[i * wq:(i + 1) * wq, :])
            acc = part if acc is None else acc + part
        o_ref[...] = x_ref[...] + mod_ref[2:3, :] * acc

    is_ctx = pl.program_id(0) < n_ctx_tiles
    pl.when(is_ctx)(lambda: run(ctx_refs))
    pl.when(jnp.logical_not(is_ctx))(lambda: run(lat_refs))


def _outproj_call(x, mixes_ctx, mixes_lat, w_out, layer, mod, mc, sd, tm):
    m, d = x.shape
    wq = mixes_ctx[0].shape[1]
    nct = mc // tm
    midx = functools.partial(_mod_row_index, tm=tm, mc=mc, sd=sd)
    cspec = pl.BlockSpec((tm, wq), lambda i: (jnp.minimum(i, nct - 1), 0))
    lspec = pl.BlockSpec((tm, wq), lambda i: (jnp.maximum(i - nct, 0), 0))
    n_mix = len(mixes_ctx)
    return pl.pallas_call(
        functools.partial(_outproj_kernel, n_ctx_tiles=nct),
        grid=(m // tm,),
        in_specs=[pl.BlockSpec((tm, d), lambda i: (i, 0))] + [cspec] * n_mix + [lspec] * n_mix
                 + [pl.BlockSpec((None, d, d), lambda i: (layer, 0, 0)),
                    pl.BlockSpec((None, 6, d), lambda i: (midx(i), 0, 0))],
        out_specs=pl.BlockSpec((tm, d), lambda i: (i, 0)),
        out_shape=jax.ShapeDtypeStruct((m, d), F32),
        compiler_params=_cparams("parallel"),
        name="out_proj",
    )(x, *mixes_ctx, *mixes_lat, w_out, mod)


def _ffn_kernel(x_ref, nw_ref, mod_ref, wg_ref, wu_ref, wd_ref, o_ref, h_ref, acc_ref):
    j = pl.program_id(1)

    @pl.when(j == 0)
    def _():
        h = _rms(x_ref[...], nw_ref[...]) * (1.0 + mod_ref[4:5, :]) + mod_ref[3:4, :]
        h_ref[...] = h.astype(BF16)
        acc_ref[...] = jnp.zeros_like(acc_ref)

    h = h_ref[...]
    a = (_silu(_dot(h, wg_ref[...])) * _dot(h, wu_ref[...])).astype(BF16)
    acc_ref[...] += _dot(a, wd_ref[...])

    @pl.when(j == pl.num_programs(1) - 1)
    def _():
        o_ref[...] = x_ref[...] + mod_ref[5:6, :] * acc_ref[...]


def _ffn_call(x, nw, mod, wg, wu, wd, layer, mc, sd, tm, tf):
    m, d = x.shape
    f = wg.shape[2]
    midx = functools.partial(_mod_row_index, tm=tm, mc=mc, sd=sd)
    return pl.pallas_call(
        _ffn_kernel,
        grid=(m // tm, f // tf),
        in_specs=[pl.BlockSpec((tm, d), lambda i, j: (i, 0)),
                  pl.BlockSpec((1, d), lambda i, j: (0, 0)),
                  pl.BlockSpec((None, 6, d), lambda i, j: (midx(i), 0, 0)),
                  pl.BlockSpec((None, d, tf), lambda i, j: (layer, 0, j)),
                  pl.BlockSpec((None, d, tf), lambda i, j: (layer, 0, j)),
                  pl.BlockSpec((None, tf, d), lambda i, j: (layer, j, 0))],
        out_specs=pl.BlockSpec((tm, d), lambda i, j: (i, 0)),
        out_shape=jax.ShapeDtypeStruct((m, d), F32),
        scratch_shapes=[pltpu.VMEM((tm, d), BF16), pltpu.VMEM((tm, d), F32)],
        compiler_params=_cparams("parallel", "arbitrary"),
        name="ffn_dense",
    )(x, nw, mod, wg, wu, wd)


def _route_kernel(x_ref, nw_ref, mod_ref, rw_ref, h_ref, r_ref, *, n_experts):
    tm = x_ref.shape[0]
    h = _rms(x_ref[...], nw_ref[...]) * (1.0 + mod_ref[4:5, :]) + mod_ref[3:4, :]
    for cb in range(ROW_CHUNKS):
        h_ref[pl.ds(cb, tm, stride=ROW_CHUNKS), :] = h[:, cb * LANES:(cb + 1) * LANES]
    lane = lax.broadcasted_iota(jnp.int32, (tm, LANES), 1).astype(F32)
    neg = jnp.float32(-jnp.inf)
    logits = jnp.where(lane < n_experts, _dot3(h, rw_ref[...]), neg)
    m1 = jnp.max(logits, axis=-1, keepdims=True)
    i1 = jnp.min(jnp.where(logits == m1, lane, float(LANES)), axis=-1, keepdims=True)
    rest = jnp.where(lane == i1, neg, logits)
    m2 = jnp.max(rest, axis=-1, keepdims=True)
    i2 = jnp.min(jnp.where(rest == m2, lane, float(LANES)), axis=-1, keepdims=True)
    e2 = jnp.exp(m2 - m1)
    den = 1.0 + e2
    r_ref[...] = jnp.where(lane == 0.0, i1,
                           jnp.where(lane == 1.0, i2,
                                     jnp.where(lane == 2.0, 1.0 / den, jnp.where(lane == 3.0, e2 / den, 0.0))))


def _route_call(x, nw, mod, rw, mc, sd, tm, n_experts):
    m, d = x.shape
    midx = functools.partial(_mod_row_index, tm=tm, mc=mc, sd=sd)
    return pl.pallas_call(
        functools.partial(_route_kernel, n_experts=n_experts),
        grid=(m // tm,),
        in_specs=[pl.BlockSpec((tm, d), lambda i: (i, 0)),
                  pl.BlockSpec((1, d), lambda i: (0, 0)),
                  pl.BlockSpec((None, 6, d), lambda i: (midx(i), 0, 0)),
                  pl.BlockSpec((d, LANES), lambda i: (0, 0))],
        out_specs=[pl.BlockSpec((tm * ROW_CHUNKS, LANES), lambda i: (i, 0)),
                   pl.BlockSpec((tm, LANES), lambda i: (i, 0))],
        out_shape=[jax.ShapeDtypeStruct((m * ROW_CHUNKS, LANES), F32),
                   jax.ShapeDtypeStruct((m, LANES), F32)],
        compiler_params=_cparams("parallel"),
        name="moe_route",
    )(x, nw, mod, rw)


def _row_copy(src_hbm, dst_ref, src_row, dst_row, sem):
    s0 = pl.multiple_of(src_row * ROW_CHUNKS, ROW_CHUNKS)
    d0 = pl.multiple_of(dst_row * ROW_PITCH, 8)
    return pltpu.make_async_copy(src_hbm.at[pl.ds(s0, ROW_CHUNKS), :], dst_ref.at[pl.ds(d0, ROW_CHUNKS), :], sem)


def _gather_kernel(nv_ref, idx_ref, src_hbm, o_ref, buf_ref, sem, *, rows):
    blk = pl.program_id(0)
    n_valid = nv_ref[blk]

    @pl.when(blk == 0)
    def _():
        buf_ref[...] = jnp.zeros_like(buf_ref)

    def start(r, carry):
        _row_copy(src_hbm, buf_ref, idx_ref[0, r], r, sem).start()
        return carry

    def wait(r, carry):
        _row_copy(src_hbm, buf_ref, idx_ref[0, r], r, sem).wait()
        return carry

    lax.fori_loop(0, n_valid, start, 0)
    lax.fori_loop(0, n_valid, wait, 0)
    keep = lax.broadcasted_iota(jnp.int32, (rows, LANES), 0) < n_valid
    for cb in range(ROW_CHUNKS):
        val = buf_ref[pl.ds(cb, rows, stride=ROW_PITCH), :]
        o_ref[:, cb * LANES:(cb + 1) * LANES] = jnp.where(keep, val, 0.0).astype(o_ref.dtype)


def _gather_call(h_rows, src_idx, n_valid, rows):
    p = src_idx.shape[0]
    d = ROW_CHUNKS * LANES
    return pl.pallas_call(
        functools.partial(_gather_kernel, rows=rows),
        grid_spec=pltpu.PrefetchScalarGridSpec(
            num_scalar_prefetch=1,
            grid=(p // rows,),
            in_specs=[pl.BlockSpec((None, 1, rows), lambda i, n: (i, 0, 0), memory_space=pltpu.SMEM),
                      pl.BlockSpec(memory_space=pl.ANY)],
            out_specs=pl.BlockSpec((rows, d), lambda i, n: (i, 0)),
            scratch_shapes=[pltpu.VMEM((rows * ROW_PITCH, LANES), F32), pltpu.SemaphoreType.DMA(())]),
        out_shape=jax.ShapeDtypeStruct((p, d), BF16),
        compiler_params=_cparams("arbitrary"),
        name="moe_gather",
    )(n_valid, src_idx.reshape(p // rows, 1, rows), h_rows)


def _experts_kernel(te_ref, nt_ref, ns_ref, x_ref, wg_ref, wu_ref, wd_ref, o_ref, acc_ref, *, sub):
    i, j = pl.program_id(0), pl.program_id(1)
    tm = x_ref.shape[0]
    last_j = pl.num_programs(1) - 1
    used = i < nt_ref[0]

    for k in range(1, tm // sub + 1):
        @pl.when(used & (ns_ref[i] == k))
        def _():
            rows = k * sub
            x = x_ref[:rows, :]
            a = (_silu(_dot(x, wg_ref[...].astype(BF16))) * _dot(x, wu_ref[...].astype(BF16))).astype(BF16)
            part = _dot(a, wd_ref[...].astype(BF16))

            @pl.when(j == 0)
            def _():
                acc_ref[:rows, :] = part
                if rows < tm:
                    acc_ref[rows:, :] = jnp.zeros((tm - rows, acc_ref.shape[1]), F32)

            @pl.when(j != 0)
            def _():
                acc_ref[:rows, :] += part

    @pl.when(used & (j == last_j))
    def _():
        for cb in range(ROW_CHUNKS):
            o_ref[pl.ds(cb, tm, stride=ROW_CHUNKS), :] = acc_ref[:, cb * LANES:(cb + 1) * LANES]

    @pl.when(jnp.logical_not(used) & (j == last_j))
    def _():
        o_ref[...] = jnp.zeros_like(o_ref)


def _experts_call(tile_e, ntiles, nsub, xs, wg, wu, wd, layer, tm, tf, sub):
    d, f = wg.shape[2:]
    p = xs.shape[0]
    nf = f // tf

    def row_blk(i, j, te, nt, ns):
        return (jnp.minimum(i, nt[0] - 1), 0)

    def jj(i, j, nt):
        return jnp.where(i < nt[0], j, nf - 1)

    def w_in_blk(i, j, te, nt, ns):
        return (layer, te[i], 0, jj(i, j, nt))

    def w_out_blk(i, j, te, nt, ns):
        return (layer, te[i], jj(i, j, nt), 0)

    return pl.pallas_call(
        functools.partial(_experts_kernel, sub=sub),
        grid_spec=pltpu.PrefetchScalarGridSpec(
            num_scalar_prefetch=3,
            grid=(p // tm, nf),
            in_specs=[pl.BlockSpec((tm, d), row_blk),
                      pl.BlockSpec((None, None, d, tf), w_in_blk),
                      pl.BlockSpec((None, None, d, tf), w_in_blk),
                      pl.BlockSpec((None, None, tf, d), w_out_blk)],
            out_specs=pl.BlockSpec((tm * ROW_CHUNKS, LANES), lambda i, j, te, nt, ns: (i, 0)),
            scratch_shapes=[pltpu.VMEM((tm, d), F32)]),
        out_shape=jax.ShapeDtypeStruct((p * ROW_CHUNKS, LANES), F32),
        compiler_params=_cparams("arbitrary", "arbitrary"),
        name="moe_experts",
    )(tile_e, ntiles, nsub, xs, wg, wu, wd)


def _combine_kernel(idx_ref, x_ref, r_ref, mod_ref, ys_hbm, o_ref, buf_ref, sem):
    tc = x_ref.shape[0]
    n_rows = TOP_K * tc

    def start(p, carry):
        for u in range(2):
            r = 2 * p + u
            _row_copy(ys_hbm, buf_ref, idx_ref[0, r], r, sem).start(priority=u)
        return carry

    def wait(r, carry):
        _row_copy(ys_hbm, buf_ref, idx_ref[0, r], r, sem).wait()
        return carry

    lax.fori_loop(0, n_rows // 2, start, 0, unroll=4)
    lax.fori_loop(0, n_rows, wait, 0, unroll=8)
    w1 = r_ref[:, 2:3]
    w2 = r_ref[:, 3:4]
    for cb in range(ROW_CHUNKS):
        cs = slice(cb * LANES, (cb + 1) * LANES)
        y1 = buf_ref[pl.ds(cb, tc, stride=ROW_PITCH), :]
        y2 = buf_ref[pl.ds(tc * ROW_PITCH + cb, tc, stride=ROW_PITCH), :]
        o_ref[:, cs] = x_ref[:, cs] + mod_ref[5:6, cs] * (y1 * w1 + y2 * w2)


def _combine_call(dest, x, route, mod, ys, mc, sd, tc):
    m, d = x.shape
    midx = functools.partial(_mod_row_index, tm=tc, mc=mc, sd=sd)
    return pl.pallas_call(
        _combine_kernel,
        grid=(m // tc,),
        in_specs=[pl.BlockSpec((None, 1, TOP_K * tc), lambda i: (i, 0, 0), memory_space=pltpu.SMEM),
                  pl.BlockSpec((tc, d), lambda i: (i, 0)),
                  pl.BlockSpec((tc, LANES), lambda i: (i, 0)),
                  pl.BlockSpec((None, 6, d), lambda i: (midx(i), 0, 0)),
                  pl.BlockSpec(memory_space=pl.ANY)],
        out_specs=pl.BlockSpec((tc, d), lambda i: (i, 0)),
        out_shape=jax.ShapeDtypeStruct((m, d), F32),
        scratch_shapes=[pltpu.VMEM((TOP_K * tc * ROW_PITCH, LANES), F32), pltpu.SemaphoreType.DMA(())],
        compiler_params=_cparams("arbitrary"),
        name="moe_combine",
    )(dest, x, route, mod, ys)


def _moe_plan(route, n_experts, tm, sub, n_tiles, tc, gather_rows):
    m = route.shape[0]
    e_flat = jnp.concatenate([route[:, 0], route[:, 1]]).astype(jnp.int32)
    onehot = (e_flat[:, None] == jnp.arange(n_experts, dtype=jnp.int32)[None, :]).astype(jnp.int32)
    csum = jnp.cumsum(onehot, axis=0)
    cnt = csum[-1]
    rank = jnp.take_along_axis(csum, e_flat[:, None], axis=1)[:, 0] - 1
    gsz = ((cnt + tm - 1) // tm) * tm
    off_end = jnp.cumsum(gsz)
    off = off_end - gsz
    dest = off[e_flat] + rank
    n_used = off_end[-1] // tm

    def owner(starts):
        return jnp.minimum(jnp.sum((off_end[None, :] <= starts[:, None]).astype(jnp.int32), axis=1), n_experts - 1)

    tiles = jnp.arange(n_tiles, dtype=jnp.int32)
    tile_e = owner(tiles * tm)
    rows_left = (off + cnt)[tile_e] - tiles * tm
    n_sub = jnp.clip((rows_left + sub - 1) // sub, 1, tm // sub)
    tile_e = jnp.where(tiles < n_used, tile_e, tile_e[jnp.maximum(n_used - 1, 0)])
    blocks = jnp.arange(n_tiles * tm // gather_rows, dtype=jnp.int32) * gather_rows
    n_valid = jnp.where(blocks < off_end[-1], jnp.clip((off + cnt)[owner(blocks)] - blocks, 0, gather_rows), 0)
    tok = jnp.tile(jnp.arange(m, dtype=jnp.int32), TOP_K)
    src = jnp.zeros((n_tiles * tm,), jnp.int32).at[dest].set(tok)
    dest_tiles = jnp.concatenate([dest[:m].reshape(m // tc, 1, tc), dest[m:].reshape(m // tc, 1, tc)], axis=2)
    i32 = lambda a: a.astype(jnp.int32)
    return src, dest_tiles, i32(tile_e), i32(n_used.reshape(1)), i32(n_sub), i32(n_valid)


def _moe_layer(x, nw, mod, rw, wg, wu, wd, layer, mc, sd, tm_route, tm, sub, tf, tc, gather_rows):
    m, d = x.shape
    n_experts = wg.shape[1]
    assert tm % gather_rows == 0 and tm % sub == 0
    n_tiles = -(-(TOP_K * m) // tm) + n_experts
    h_rows, route = _route_call(x, nw, mod, rw, mc, sd, tm_route, n_experts)
    src, dest_tiles, tile_e, n_used, n_sub, n_valid = _moe_plan(route, n_experts, tm, sub, n_tiles, tc, gather_rows)
    xs = _gather_call(h_rows, src, n_valid, gather_rows)
    ys = _experts_call(tile_e, n_used, n_sub, xs, wg, wu, wd, layer, tm, tf, sub)
    return _combine_call(dest_tiles, x, route, mod, ys, mc, sd, tc)


def _final_kernel(x_ref, w_ref, o_ref):
    o_ref[...] = _rms(x_ref[...], w_ref[...])


def _final_call(x, w, row0, rows, tm):
    d = x.shape[1]
    blk0 = row0 // tm
    return pl.pallas_call(
        _final_kernel,
        grid=(rows // tm,),
        in_specs=[pl.BlockSpec((tm, d), lambda i: (blk0 + i, 0)), pl.BlockSpec((1, d), lambda i: (0, 0))],
        out_specs=pl.BlockSpec((tm, d), lambda i: (i, 0)),
        out_shape=jax.ShapeDtypeStruct((rows, d), F32),
        compiler_params=_cparams("parallel"),
        name="final_norm",
    )(x, w)


def _sincos_2d(rows, cols, dim):
    quarter = dim // 4
    omega = 1.0 / (10000.0 ** (jnp.arange(quarter, dtype=F32) / quarter))
    r = jnp.arange(rows, dtype=F32)[:, None] * omega
    cc = jnp.arange(cols, dtype=F32)[:, None] * omega
    r_emb = jnp.concatenate([jnp.sin(r), jnp.cos(r)], axis=-1)
    c_emb = jnp.concatenate([jnp.sin(cc), jnp.cos(cc)], axis=-1)
    emb = jnp.concatenate([jnp.broadcast_to(r_emb[:, None, :], (rows, cols, dim // 2)),
                           jnp.broadcast_to(c_emb[None, :, :], (rows, cols, dim // 2))], axis=-1)
    return emb.reshape(rows * cols, dim)


def _pad_lanes(a):
    return jnp.pad(a, ((0, 0), (0, LANES - a.shape[1])))


def kernel(x_prompt, x_sample, c, state_delta, c_ctx, norm1_w, norm2_w, w_mod, b_mod, w_in, w_out, qkv_conv_w, delta_a_log, delta_dt_bias, delta_norm_w, sgu_norm_w, sgu_w, sgu_b, sconv_w, pool_w, pool_scale, ffn_w_gate, ffn_w_up, ffn_w_down, router_w, moe_w_gate, moe_w_up, moe_w_down, final_norm_w):
    bc, sc, d = x_prompt.shape
    bd, sd, _ = x_sample.shape
    depth = w_in.shape[0]
    mc, ml = bc * sc, bd * sd
    m = mc + ml
    w_a = H_A * LANES
    w_b, w_c, w_d = sgu_norm_w.shape[1], sconv_w.shape[1], pool_scale.shape[1]
    assert d == ROW_CHUNKS * LANES and mc % sd == 0 and w_a == w_b == w_c == w_d == d // 4
    assert delta_a_log.shape[1:] == (2, H_A) and delta_norm_w.shape[1] == LANES

    tm_mm = 512 if m % 512 == 0 else 256
    grid_w = 64
    pos = _sincos_2d(sd // grid_w, grid_w, d)
    x = jnp.concatenate([x_prompt.reshape(mc, d), (x_sample + pos[None]).reshape(ml, d)], axis=0)

    rm = -(-(1 + bd) // 8) * 8
    cond = jnp.zeros((rm, d), F32).at[0].set(c_ctx).at[1:1 + bd].set(c)
    mod_all = _mod_call(cond, w_mod, b_mod).reshape(depth, rm, 6, d)

    o0 = 3 * w_a
    o1 = o0 + w_a
    o3 = o1 + 4 * H_A
    o4 = o3 + 2 * w_b
    o5 = o4 + 3 * w_c
    blk0 = mc // sd
    w_main_all = jnp.concatenate([w_in[:, :, :o1], w_in[:, :, o3:]], axis=2).astype(BF16)
    w_ab_all = jnp.pad(w_in[:, :, o1:o3], ((0, 0), (0, 0), (0, LANES - (o3 - o1)))).astype(BF16)
    w_out_b = w_out.astype(BF16)
    ffn_b = [w.astype(BF16) for w in (ffn_w_gate, ffn_w_up, ffn_w_down)]
    moe_w = (moe_w_gate, moe_w_up, moe_w_down)
    ctx_states = []
    for l in range(depth):
        mod = mod_all[l]
        tm_in = 1024 if (mc % 1024 == 0 and sd % 1024 == 0) else tm_mm
        proj, ab = _inproj_call(x, norm1_w[l][None], mod, w_main_all, w_ab_all, l, mc, sd, tm_in, 1024)

        conv_t = qkv_conv_w[l].T
        alog = _pad_lanes(delta_a_log[l].reshape(1, 2 * H_A))
        dtb = _pad_lanes(delta_dt_bias[l].reshape(1, 2 * H_A))
        nwa = delta_norm_w[l][None]
        a_ctx, s_ctx = _delta_call(proj, ab, conv_t, alog, dtb, nwa, None, l,
                                   t_len=sc, nseq=bc, blk0=0, write_state=True)
        (a_lat,) = _delta_call(proj, ab, conv_t, alog, dtb, nwa, state_delta, l,
                               t_len=sd, nseq=bd, blk0=blk0, write_state=False)
        ctx_states.append(s_ctx)

        def both_paths(kern, name, col_block, width_in, params, pspecs, width_out):
            return (_seq_call(kern, name, proj, col_block, width_in, params, pspecs,
                              t_len=sc, nseq=bc, blk0=0, width_out=width_out),
                    _seq_call(kern, name, proj, col_block, width_in, params, pspecs,
                              t_len=sd, nseq=bd, blk0=blk0, width_out=width_out))

        n_g = sgu_w.shape[1]
        b_ctx, b_lat = both_paths(
            _sgu_kernel, "sgu", o1 // (2 * w_b), 2 * w_b,
            [sgu_norm_w[l][None], sgu_w[l].astype(BF16), sgu_b[l].reshape(n_g, SGU_CHUNK, 1)],
            [pl.BlockSpec((1, w_b), lambda b: (0, 0)),
             pl.BlockSpec((n_g, SGU_CHUNK, SGU_CHUNK), lambda b: (0, 0, 0)),
             pl.BlockSpec((n_g, SGU_CHUNK, 1), lambda b: (0, 0, 0))], w_b)
        c_ctx_mix, c_lat = both_paths(
            _sconv_kernel, "sconv", (o1 + 2 * w_b) // (3 * w_c), 3 * w_c,
            [sconv_w[l].T], [pl.BlockSpec((3, w_c), lambda b: (0, 0))], w_c)
        n_gd = pool_w.shape[1]
        d_ctx, d_lat = both_paths(
            _pool_kernel, "pool", (o1 + 2 * w_b + 3 * w_c) // w_d, w_d,
            [pool_w[l].astype(BF16), pool_scale[l][None]],
            [pl.BlockSpec((n_gd, LANES, LANES), lambda b: (0, 0, 0)),
             pl.BlockSpec((1, w_d), lambda b: (0, 0))], w_d)

        x = _outproj_call(x, (a_ctx, b_ctx, c_ctx_mix, d_ctx), (a_lat, b_lat, c_lat, d_lat),
                          w_out_b, l, mod, mc, sd, tm_mm)

        jl = l // 2
        if l % 2 == 0:
            x = _ffn_call(x, norm2_w[l][None], mod, *ffn_b, jl, mc, sd, tm_mm, 512)
        else:
            x = _moe_layer(x, norm2_w[l][None], mod, _pad_lanes(router_w[jl]), *moe_w, jl,
                           mc, sd, tm_route=256, tm=640, sub=128, tf=512, tc=256, gather_rows=320)

    y_prompt = _final_call(x, final_norm_w[None], 0, mc, tm_mm).reshape(bc, sc, d)
    y_sample = _final_call(x, final_norm_w[None], mc, ml, tm_mm).reshape(bd, sd, d)
    new_state = jnp.stack(ctx_states, axis=1).astype(x_prompt.dtype)
    return (y_prompt, y_sample, new_state)
```
